```python
import jax, jax.numpy as jnp
from jax import lax
import numpy as np

D_MODEL = 1024
BATCH = 2
SEQ = 8192
DEPTH = 4
DEC_BATCH = 128
DEC_SEQ = 4
PAST_LEN = 2048
PAGE_SIZE = 128

D_RNN = D_MODEL // 2
LRU_BLOCKS = 8
LRU_BLOCK = D_RNN // LRU_BLOCKS
LRU_C = 8.0
CONV_A = 4
SB_HEADS = 8
SB_HEAD_DIM = (D_MODEL // 2) // SB_HEADS
SB_WIDTH = SB_HEADS * SB_HEAD_DIM
Q_BLOCK = 128
SB_BIAS_LO = -8.0
SB_BIAS_HI = -4.0
DN_HEADS = 8
DN_DK = 64
DN_DV = 128
DN_KW = DN_HEADS * DN_DK
DN_VW = DN_HEADS * DN_DV
DN_CONV_CH = 2 * DN_KW + DN_VW
CONV_C = 4
DN_CHUNK = 64
D_FF = 4 * D_MODEL
CONV_F = 3

N_EVEN = (DEPTH + 1) // 2
N_ODD = DEPTH // 2
E_IN = 2 * D_RNN + 3 * SB_WIDTH
O_IN = DN_CONV_CH + DN_VW + 2 * DN_HEADS
EPS = 1e-6
F32 = jnp.float32

kernel_name = 'hybrid_rglru_stickbreak_gdn_convffn_step'


def rmsnorm(x, w):
    xf = x.astype(F32)
    y = xf * lax.rsqrt(jnp.mean(xf * xf, axis=-1, keepdims=True) + EPS)
    return (y * w.astype(F32)).astype(x.dtype)


def causal_dwconv(x, buf, w, b=None):
    W = w.shape[0]
    T = x.shape[1]
    xp = jnp.concatenate([buf.astype(x.dtype), x], axis=1)
    y = xp[:, 0:T] * w[0]
    for i in range(1, W):
        y = y + xp[:, i:i + T] * w[i]
    if b is not None:
        y = y + b
    return y, xp[:, xp.shape[1] - (W - 1):]


def rg_lru(x, h0, pos, wa, ba, wx, bx, lam):
    B, T, _ = x.shape
    xf = x.astype(F32)
    xb = xf.reshape(B, T, LRU_BLOCKS, LRU_BLOCK)
    r = jax.nn.sigmoid(jnp.einsum('btni,nij->btnj', xb, wa.astype(F32)).reshape(B, T, D_RNN) + ba.astype(F32))
    gi = jax.nn.sigmoid(jnp.einsum('btni,nij->btnj', xb, wx.astype(F32)).reshape(B, T, D_RNN) + bx.astype(F32))
    log_a = -LRU_C * r * jax.nn.softplus(-lam.astype(F32))
    a = jnp.exp(log_a)
    mult = jnp.sqrt(-jnp.expm1(2.0 * log_a))
    mult = jnp.where((pos == 0)[None, :, None], 1.0, mult)
    b = xf * gi * mult
    b = b.at[:, 0].add(a[:, 0] * h0.astype(F32))

    def comb(e1, e2):
        a1, b1 = e1
        a2, b2 = e2
        return a1 * a2, a2 * b1 + b2

    _, h = lax.associative_scan(comb, (a, b), axis=1)
    return h, h[:, -1]


def stick_breaking(q, k, v, q_pos, k_pos, bias):
    z = jnp.einsum('bqhd,bkhd->bhqk', q.astype(F32), k.astype(F32)) * (SB_HEAD_DIM ** -0.5)
    z = z + bias.astype(F32)[None, :, None, None]
    mask = k_pos[None, :] < q_pos[:, None]
    log_beta = jax.nn.log_sigmoid(z)
    log_keep = jnp.where(mask, log_beta - z, 0.0)
    later = lax.cumsum(log_keep, axis=3, reverse=True) - log_keep
    A = jnp.where(mask, jnp.exp(log_beta + later), 0.0)
    return jnp.einsum('bhqk,bkhd->bqhd', A, v.astype(F32)).astype(q.dtype)


def stick_breaking_prompt(q, k, v, bias):
    B, T, H, Dh = q.shape
    nb = T // Q_BLOCK
    qb = jnp.moveaxis(q.reshape(B, nb, Q_BLOCK, H, Dh), 1, 0)
    pb = jnp.arange(T).reshape(nb, Q_BLOCK)
    k_pos = jnp.arange(T)
    out = lax.map(lambda a: stick_breaking(a[0], k, v, a[1], k_pos, bias), (qb, pb))
    return jnp.moveaxis(out, 0, 1).reshape(B, T, H, Dh)


def gather_pages(pool, page_table):
    g = pool[page_table]
    return g.reshape(g.shape[0], g.shape[1] * g.shape[2], g.shape[3], g.shape[4])


def even_mixer(x, pos, h0, lru_buf, past, w_in, conv_w, conv_b, wa, ba, wx, bx, lam, sb_bias, w_out):
    B, T, _ = x.shape
    proj = x @ w_in
    xr, gr, q, k, v = jnp.split(proj, [D_RNN, 2 * D_RNN, 2 * D_RNN + SB_WIDTH, 2 * D_RNN + 2 * SB_WIDTH], axis=-1)
    xc, new_buf = causal_dwconv(xr, lru_buf, conv_w, conv_b)
    h, h_last = rg_lru(xc, h0, pos, wa, ba, wx, bx, lam)
    y_rnn = (h * jax.nn.gelu(gr.astype(F32))).astype(x.dtype)
    q = q.reshape(B, T, SB_HEADS, SB_HEAD_DIM)
    k = k.reshape(B, T, SB_HEADS, SB_HEAD_DIM)
    v = v.reshape(B, T, SB_HEADS, SB_HEAD_DIM)
    if past is None:
        y_att = stick_breaking_prompt(q, k, v, sb_bias)
    else:
        k_past, v_past = past
        P = k_past.shape[1]
        k_all = jnp.concatenate([k_past.astype(k.dtype), k], axis=1)
        v_all = jnp.concatenate([v_past.astype(v.dtype), v], axis=1)
        k_pos = jnp.concatenate([jnp.arange(P), pos])
        y_att = stick_breaking(q, k_all, v_all, pos, k_pos, sb_bias)
    y = jnp.concatenate([y_rnn, y_att.reshape(B, T, SB_WIDTH).astype(x.dtype)], axis=-1) @ w_out
    return y, new_buf, h_last.astype(x.dtype), k, v


def gated_delta_rule(q, k, v, g, beta, S0):
    B, T, H, Dk = q.shape
    Dv = v.shape[-1]
    C = min(DN_CHUNK, T)
    pad = (-T) % C
    N = (T + pad) // C

    def prep(a):
        a = jnp.moveaxis(a.astype(F32), 2, 1)
        a = jnp.pad(a, [(0, 0), (0, 0), (0, pad)] + [(0, 0)] * (a.ndim - 3))
        a = a.reshape((B, H, N, C) + a.shape[3:])
        return jnp.moveaxis(a, 2, 0)

    q = prep(q.astype(F32) * (Dk ** -0.5))
    k = prep(k)
    v = prep(v)
    g = prep(g)
    beta = prep(beta)
    gc = jnp.cumsum(g, axis=-1)
    incl = jnp.tril(jnp.ones((C, C), dtype=bool))
    strict = jnp.tril(jnp.ones((C, C), dtype=bool), -1)
    diff = gc[..., :, None] - gc[..., None, :]
    decay = jnp.where(incl, jnp.exp(jnp.where(incl, diff, 0.0)), 0.0)
    kb = k * beta[..., None]
    Lm = jnp.where(strict, jnp.einsum('nbhcd,nbhsd->nbhcs', kb, k) * decay, 0.0)
    rhs = jnp.concatenate([v * beta[..., None], kb * jnp.exp(gc)[..., None]], axis=-1)
    sol = lax.linalg.triangular_solve(Lm + jnp.eye(C, dtype=F32), rhs, left_side=True, lower=True)
    u, w = sol[..., :Dv], sol[..., Dv:]

    def step(S, xs):
        qn, kn, un, wn, gn, dn = xs
        v_new = un - jnp.einsum('bhcd,bhde->bhce', wn, S)
        attn = jnp.einsum('bhcd,bhsd->bhcs', qn, kn) * dn
        o = jnp.einsum('bhcd,bhde->bhce', qn * jnp.exp(gn)[..., None], S) + jnp.einsum('bhcs,bhse->bhce', attn, v_new)
        g_last = gn[..., -1:]
        S = S * jnp.exp(g_last)[..., None] + jnp.einsum('bhcd,bhce->bhde', kn * jnp.exp(g_last - gn)[..., None], v_new)
        return S, o

    S, o = lax.scan(step, S0.astype(F32), (q, k, u, w, gc, decay))
    o = jnp.moveaxis(o, 0, 2).reshape(B, H, N * C, Dv)[:, :, :T]
    return jnp.moveaxis(o, 1, 2), S


def odd_mixer(x, S0, conv_buf, w_in, conv_w, A_log, dt_bias, norm_w, w_out):
    B, T, _ = x.shape
    proj = x @ w_in
    qkv, z, b, a = jnp.split(proj, [DN_CONV_CH, DN_CONV_CH + DN_VW, DN_CONV_CH + DN_VW + DN_HEADS], axis=-1)
    qkv, new_buf = causal_dwconv(qkv, conv_buf, conv_w)
    qkv = jax.nn.silu(qkv.astype(F32))
    q, k, v = jnp.split(qkv, [DN_KW, 2 * DN_KW], axis=-1)
    q = q.reshape(B, T, DN_HEADS, DN_DK)
    k = k.reshape(B, T, DN_HEADS, DN_DK)
    v = v.reshape(B, T, DN_HEADS, DN_DV)
    q = q * lax.rsqrt(jnp.sum(q * q, axis=-1, keepdims=True) + EPS)
    k = k * lax.rsqrt(jnp.sum(k * k, axis=-1, keepdims=True) + EPS)
    beta = jax.nn.sigmoid(b.astype(F32))
    g = -jnp.exp(A_log.astype(F32)) * jax.nn.softplus(a.astype(F32) + dt_bias.astype(F32))
    o, S = gated_delta_rule(q, k, v, g, beta, S0)
    o = rmsnorm(o, norm_w) * jax.nn.silu(z.astype(F32).reshape(B, T, DN_HEADS, DN_DV))
    y = o.reshape(B, T, DN_VW).astype(x.dtype) @ w_out
    return y, new_buf, S.astype(x.dtype)


def conv_ffn(x, buf, w_up, conv_w, conv_b, w_down):
    u = x @ w_up
    uc, new_buf = causal_dwconv(u, buf, conv_w, conv_b)
    gate, val = jnp.split(uc, 2, axis=-1)
    return (jax.nn.gelu(gate) * val) @ w_down, new_buf


def trunk(x, pos, lru_h, lru_conv, dn_S, dn_conv, ffn_conv, sb_past, p):
    ks, vs, hs, lcs, Ss, dcs, fcs = [], [], [], [], [], [], []
    for l in range(DEPTH):
        j = l // 2
        xn = rmsnorm(x, p['norm_mix_pre'][l])
        if l % 2 == 0:
            if sb_past is None:
                past = None
            else:
                cache_k, cache_v, page_table = sb_past
                past = (gather_pages(cache_k[j], page_table), gather_pages(cache_v[j], page_table))
            y, lc, h, k, v = even_mixer(xn, pos, lru_h[j], lru_conv[j], past, p['w_in_e'][j], p['lru_conv_w'][j],
                                        p['lru_conv_b'][j], p['lru_wa'][j], p['lru_ba'][j], p['lru_wx'][j],
                                        p['lru_bx'][j], p['lru_lambda'][j], p['sb_bias'][j], p['w_out_e'][j])
            ks.append(k)
            vs.append(v)
            hs.append(h)
            lcs.append(lc)
        else:
            y, dc, S = odd_mixer(xn, dn_S[j], dn_conv[j], p['w_in_o'][j], p['dn_conv_w'][j], p['dn_A_log'][j],
                                 p['dn_dt_bias'][j], p['dn_norm_w'][j], p['w_out_o'][j])
            Ss.append(S)
            dcs.append(dc)
        x = x + rmsnorm(y, p['norm_mix_post'][l])
        f, fc = conv_ffn(rmsnorm(x, p['norm_ffn_pre'][l]), ffn_conv[l], p['ffn_w_up'][l], p['ffn_conv_w'][l],
                         p['ffn_conv_b'][l], p['ffn_w_down'][l])
        x = x + rmsnorm(f, p['norm_ffn_post'][l])
        fcs.append(fc)
    return (x, jnp.stack(ks), jnp.stack(vs), jnp.stack(hs), jnp.stack(lcs), jnp.stack(Ss), jnp.stack(dcs),
            jnp.stack(fcs))


def setup_inputs(seed: int = 0) -> dict:
    key = jax.random.key(seed)
    kit = iter(list(jax.random.split(key, 40)))

    def nrm(shape, scale):
        return jax.random.normal(next(kit), shape, F32) * scale

    def unif(shape, lo, hi):
        return jax.random.uniform(next(kit), shape, F32, lo, hi)

    n_pages = PAST_LEN // PAGE_SIZE
    n_used = DEC_BATCH * n_pages
    n_pool = n_used + (n_used + 3) // 4
    x_prompt = nrm((BATCH, SEQ, D_MODEL), 1.0)
    x_sample = nrm((DEC_BATCH, DEC_SEQ, D_MODEL), 1.0)
    cache_sb_k = nrm((N_EVEN, n_pool, PAGE_SIZE, SB_HEADS, SB_HEAD_DIM), 1.0)
    cache_sb_v = nrm((N_EVEN, n_pool, PAGE_SIZE, SB_HEADS, SB_HEAD_DIM), 1.0)
    state_lru_h = nrm((N_EVEN, DEC_BATCH, D_RNN), 0.5)
    state_lru_conv = nrm((N_EVEN, DEC_BATCH, CONV_A - 1, D_RNN), 1.0)
    state_dn_S = nrm((N_ODD, DEC_BATCH, DN_HEADS, DN_DK, DN_DV), 0.3)
    state_dn_conv = nrm((N_ODD, DEC_BATCH, CONV_C - 1, DN_CONV_CH), 1.0)
    state_ffn_conv = nrm((DEPTH, DEC_BATCH, CONV_F - 1, 2 * D_FF), 1.0)
    page_table = jax.random.permutation(next(kit), n_pool)[:n_used].reshape(DEC_BATCH, n_pages).astype(jnp.int32)
    norm_mix_pre = 1.0 + nrm((DEPTH, D_MODEL), 0.05)
    norm_mix_post = 1.0 + nrm((DEPTH, D_MODEL), 0.05)
    norm_ffn_pre = 1.0 + nrm((DEPTH, D_MODEL), 0.05)
    norm_ffn_post = 1.0 + nrm((DEPTH, D_MODEL), 0.05)
    w_in_e = nrm((N_EVEN, D_MODEL, E_IN), D_MODEL ** -0.5)
    lru_conv_w = nrm((N_EVEN, CONV_A, D_RNN), CONV_A ** -0.5)
    lru_conv_b = nrm((N_EVEN, D_RNN), 0.01)
    lru_wa = nrm((N_EVEN, LRU_BLOCKS, LRU_BLOCK, LRU_BLOCK), LRU_BLOCK ** -0.5)
    lru_ba = nrm((N_EVEN, D_RNN), 0.01)
    lru_wx = nrm((N_EVEN, LRU_BLOCKS, LRU_BLOCK, LRU_BLOCK), LRU_BLOCK ** -0.5)
    lru_bx = nrm((N_EVEN, D_RNN), 0.01)
    s = unif((N_EVEN, D_RNN), 0.9, 0.999) ** (1.0 / LRU_C)
    lru_lambda = jnp.log(s) - jnp.log1p(-s)
    sb_bias = unif((N_EVEN, SB_HEADS), SB_BIAS_LO, SB_BIAS_HI)
    w_out_e = nrm((N_EVEN, D_RNN + SB_WIDTH, D_MODEL), (D_RNN + SB_WIDTH) ** -0.5)
    w_in_o = nrm((N_ODD, D_MODEL, O_IN), D_MODEL ** -0.5)
    dn_conv_w = nrm((N_ODD, CONV_C, DN_CONV_CH), CONV_C ** -0.5)
    dn_A_log = jnp.log(unif((N_ODD, DN_HEADS), 1.0, 16.0))
    dt = jnp.exp(unif((N_ODD, DN_HEADS), float(np.log(1e-3)), float(np.log(1e-1))))
    dn_dt_bias = dt + jnp.log(-jnp.expm1(-dt))
    dn_norm_w = 1.0 + nrm((N_ODD, DN_DV), 0.05)
    w_out_o = nrm((N_ODD, DN_VW, D_MODEL), DN_VW ** -0.5)
    ffn_w_up = nrm((DEPTH, D_MODEL, 2 * D_FF), D_MODEL ** -0.5)
    ffn_conv_w = nrm((DEPTH, CONV_F, 2 * D_FF), CONV_F ** -0.5)
    ffn_conv_b = nrm((DEPTH, 2 * D_FF), 0.01)
    ffn_w_down = nrm((DEPTH, D_FF, D_MODEL), D_FF ** -0.5)
    return {'x_prompt': x_prompt, 'x_sample': x_sample, 'cache_sb_k': cache_sb_k, 'cache_sb_v': cache_sb_v,
            'state_lru_h': state_lru_h, 'state_lru_conv': state_lru_conv, 'state_dn_S': state_dn_S,
            'state_dn_conv': state_dn_conv, 'state_ffn_conv': state_ffn_conv, 'page_table': page_table,
            'norm_mix_pre': norm_mix_pre, 'norm_mix_post': norm_mix_post, 'norm_ffn_pre': norm_ffn_pre,
            'norm_ffn_post': norm_ffn_post, 'w_in_e': w_in_e, 'lru_conv_w': lru_conv_w, 'lru_conv_b': lru_conv_b,
            'lru_wa': lru_wa, 'lru_ba': lru_ba, 'lru_wx': lru_wx, 'lru_bx': lru_bx, 'lru_lambda': lru_lambda,
            'sb_bias': sb_bias, 'w_out_e': w_out_e, 'w_in_o': w_in_o, 'dn_conv_w': dn_conv_w,
            'dn_A_log': dn_A_log, 'dn_dt_bias': dn_dt_bias, 'dn_norm_w': dn_norm_w, 'w_out_o': w_out_o,
            'ffn_w_up': ffn_w_up, 'ffn_conv_w': ffn_conv_w, 'ffn_conv_b': ffn_conv_b, 'ffn_w_down': ffn_w_down}


def reference(x_prompt, x_sample, cache_sb_k, cache_sb_v, state_lru_h, state_lru_conv, state_dn_S, state_dn_conv,
              state_ffn_conv, page_table, norm_mix_pre, norm_mix_post, norm_ffn_pre, norm_ffn_post, w_in_e,
              lru_conv_w, lru_conv_b, lru_wa, lru_ba, lru_wx, lru_bx, lru_lambda, sb_bias, w_out_e, w_in_o,
              dn_conv_w, dn_A_log, dn_dt_bias, dn_norm_w, w_out_o, ffn_w_up, ffn_conv_w, ffn_conv_b, ffn_w_down):
    p = {'norm_mix_pre': norm_mix_pre, 'norm_mix_post': norm_mix_post, 'norm_ffn_pre': norm_ffn_pre,
         'norm_ffn_post': norm_ffn_post, 'w_in_e': w_in_e, 'lru_conv_w': lru_conv_w, 'lru_conv_b': lru_conv_b,
         'lru_wa': lru_wa, 'lru_ba': lru_ba, 'lru_wx': lru_wx, 'lru_bx': lru_bx, 'lru_lambda': lru_lambda,
         'sb_bias': sb_bias, 'w_out_e': w_out_e, 'w_in_o': w_in_o, 'dn_conv_w': dn_conv_w,
         'dn_A_log': dn_A_log, 'dn_dt_bias': dn_dt_bias, 'dn_norm_w': dn_norm_w, 'w_out_o': w_out_o,
         'ffn_w_up': ffn_w_up, 'ffn_conv_w': ffn_conv_w, 'ffn_conv_b': ffn_conv_b, 'ffn_w_down': ffn_w_down}
    B, T, _ = x_prompt.shape
    dt_ = x_prompt.dtype
    pos_p = jnp.arange(T)
    (y_prompt, k_p, v_p, h_p, lc_p, S_p, dc_p, fc_p) = trunk(
        x_prompt, pos_p,
        jnp.zeros((N_EVEN, B, D_RNN), dt_), jnp.zeros((N_EVEN, B, CONV_A - 1, D_RNN), dt_),
        jnp.zeros((N_ODD, B, DN_HEADS, DN_DK, DN_DV), dt_), jnp.zeros((N_ODD, B, CONV_C - 1, DN_CONV_CH), dt_),
        jnp.zeros((DEPTH, B, CONV_F - 1, 2 * D_FF), dt_), None, p)
    past_len = page_table.shape[1] * cache_sb_k.shape[2]
    pos_s = past_len + jnp.arange(x_sample.shape[1])
    (y_sample, k_s, v_s, h_s, lc_s, S_s, dc_s, fc_s) = trunk(
        x_sample, pos_s, state_lru_h, state_lru_conv, state_dn_S, state_dn_conv, state_ffn_conv,
        (cache_sb_k, cache_sb_v, page_table), p)
    return (y_prompt, y_sample, k_p, v_p, k_s, v_s, h_p, h_s, lc_p, lc_s, S_p, S_s, dc_p, dc_s, fc_p, fc_s)
```

```python
import functools

import numpy as np
import jax
import jax.numpy as jnp
from jax import lax
from jax.experimental import pallas as pl
from jax.experimental.pallas import tpu as pltpu

F32 = jnp.float32
BF16 = jnp.bfloat16
EPS = 1e-6

D = 1024
DR = 512
SBW = 512
NH = 8
HD = 64
LRU_C = 8.0
E_IN = 2 * DR + 3 * SBW
DK = 64
DV = 128
KW = NH * DK
VW = NH * DV
CCH = 2 * KW + VW
O_IN = CCH + VW + 2 * NH
O_PAD = CCH + VW + 128
DFF = 4096
DN_CHUNK = 64
LANES = 128
VMEM_LIMIT = 52 * 1024 * 1024


def _cp(*sem):
    return pltpu.CompilerParams(dimension_semantics=sem, vmem_limit_bytes=VMEM_LIMIT)


def _rms(x, w):
    ms = jnp.mean(x * x, axis=-1, keepdims=True)
    return x * lax.rsqrt(ms + EPS) * w


def _dot(a, b):
    return jnp.dot(a.astype(BF16), b.astype(BF16), preferred_element_type=F32)


def _dot_nt(a, b):
    return lax.dot_general(a.astype(BF16), b.astype(BF16), (((1,), (1,)), ((), ())),
                           preferred_element_type=F32)


def _dot_tn(a, b):
    return lax.dot_general(a.astype(BF16), b.astype(BF16), (((0,), (0,)), ((), ())),
                           preferred_element_type=F32)


def _split(a):
    hi = a.astype(BF16)
    lo = (a - hi.astype(F32)).astype(BF16)
    return hi, lo


def _dot2(a, b_exact):
    hi, lo = _split(a)
    b = b_exact.astype(BF16)
    return (jnp.dot(hi, b, preferred_element_type=F32) + jnp.dot(lo, b, preferred_element_type=F32))


def _dot3(a, b):
    ah, al = _split(a)
    bh, bl = _split(b)
    return (jnp.dot(ah, bh, preferred_element_type=F32) + jnp.dot(ah, bl, preferred_element_type=F32)
            + jnp.dot(al, bh, preferred_element_type=F32))


def _expm1(x):
    return jnp.tanh(0.5 * x) * (jnp.exp(x) + 1.0)


def _scan_rows(a, b):
    n = a.shape[0]
    row = lax.broadcasted_iota(jnp.int32, a.shape, 0)
    s = 1
    while s < n:
        m = row >= s
        a_sh = pltpu.roll(a, s, 0)
        b_sh = pltpu.roll(b, s, 0)
        b = jnp.where(m, a * b_sh + b, b)
        a = jnp.where(m, a * a_sh, a)
        s *= 2
    return a, b


def _cumsum_rows(x):
    n = x.shape[0]
    row = lax.broadcasted_iota(jnp.int32, x.shape, 0)
    s = 1
    while s < n:
        x = x + jnp.where(row >= s, pltpu.roll(x, s, 0), 0.0)
        s *= 2
    return x


def _lru_gates(xc, wa, ba, wx, bx, lam):
    r = jax.nn.sigmoid(_dot(xc, wa) + ba)
    gi = jax.nn.sigmoid(_dot(xc, wx) + bx)
    log_a = -LRU_C * r * jax.nn.softplus(-lam)
    a = jnp.exp(log_a)
    mult = jnp.sqrt(-_expm1(2.0 * log_a))
    return a, gi, mult


def _sb_terms(z):
    l1p = jnp.log1p(jnp.exp(-jnp.abs(z)))
    log_beta = jnp.minimum(z, 0.0) - l1p
    return log_beta, log_beta - z


def _even_in_prompt_kernel(x_ref, nw_ref, win_ref, cw_ref, cb_ref, wa_ref, ba_ref, wx_ref, bx_ref, lam_ref,
                           yr_ref, q_ref, k_ref, v_ref, hl_ref, tail_ref, xbuf, hcar, *, tm):
    i = pl.program_id(1)
    xn = _rms(x_ref[...], nw_ref[...])
    proj = jnp.dot(xn.astype(BF16), win_ref[...], preferred_element_type=F32)
    xr = proj[:, 0:DR]
    gr = proj[:, DR:2 * DR]
    q_ref[...] = proj[:, 2 * DR:2 * DR + SBW]
    k_ref[...] = proj[:, 2 * DR + SBW:2 * DR + 2 * SBW]
    v_ref[...] = proj[:, 2 * DR + 2 * SBW:2 * DR + 3 * SBW]

    @pl.when(i == 0)
    def _():
        xbuf[0:8, :] = jnp.zeros((8, DR), F32)
        hcar[...] = jnp.zeros((1, DR), F32)

    @pl.when(i > 0)
    def _():
        xbuf[0:8, :] = xbuf[tm:tm + 8, :]

    xbuf[8:tm + 8, :] = xr
    cw = cw_ref[...]
    xc = (cw[3:4] * xr + cw[2:3] * xbuf[pl.ds(7, tm), :] + cw[1:2] * xbuf[pl.ds(6, tm), :]
          + cw[0:1] * xbuf[pl.ds(5, tm), :] + cb_ref[...])
    a, gi, mult = _lru_gates(xc, wa_ref[...], ba_ref[...], wx_ref[...], bx_ref[...], lam_ref[...])
    row = lax.broadcasted_iota(jnp.int32, (tm, DR), 0)
    mult = jnp.where(jnp.logical_and(row == 0, i == 0), 1.0, mult)
    b = xc * gi * mult
    pa, hb = _scan_rows(a, b)
    h = hb + pa * hcar[...]
    hcar[...] = h[tm - 1:tm, :]
    yr_ref[...] = h * jax.nn.gelu(gr)
    hl_ref[0] = h[tm - 1:tm, :]
    tail_ref[0] = xbuf[pl.ds(tm + 5, 3), :]


def _even_in_prompt(x, nw, win, cw, cb, wa, ba, wx, bx, lam, *, B, T, tm):
    nt = T // tm
    N = B * T
    row = lambda b, i: (b * nt + i, 0)
    const = lambda b, i: (0, 0)
    outs = pl.pallas_call(
        functools.partial(_even_in_prompt_kernel, tm=tm),
        grid=(B, nt),
        in_specs=[pl.BlockSpec((tm, D), row), pl.BlockSpec((1, D), const), pl.BlockSpec((D, E_IN), const),
                  pl.BlockSpec((4, DR), const), pl.BlockSpec((1, DR), const), pl.BlockSpec((DR, DR), const),
                  pl.BlockSpec((1, DR), const), pl.BlockSpec((DR, DR), const), pl.BlockSpec((1, DR), const),
                  pl.BlockSpec((1, DR), const)],
        out_specs=[pl.BlockSpec((tm, DR), row), pl.BlockSpec((tm, SBW), row), pl.BlockSpec((tm, SBW), row),
                   pl.BlockSpec((tm, SBW), row), pl.BlockSpec((1, 1, DR), lambda b, i: (b, 0, 0)),
                   pl.BlockSpec((1, 3, DR), lambda b, i: (b, 0, 0))],
        out_shape=[jax.ShapeDtypeStruct((N, DR), F32), jax.ShapeDtypeStruct((N, SBW), F32),
                   jax.ShapeDtypeStruct((N, SBW), F32), jax.ShapeDtypeStruct((N, SBW), F32),
                   jax.ShapeDtypeStruct((B, 1, DR), F32), jax.ShapeDtypeStruct((B, 3, DR), F32)],
        scratch_shapes=[pltpu.VMEM((tm + 8, DR), F32), pltpu.VMEM((1, DR), F32)],
        compiler_params=_cp("arbitrary", "arbitrary"),
        name="even_in_prompt",
    )(x, nw, win, cw, cb, wa, ba, wx, bx, lam)
    return outs


def _even_in_sample_kernel(x_ref, nw_ref, win_ref, cbuf_ref, h0_ref, cw_ref, cb_ref, wa_ref, ba_ref, wx_ref,
                           bx_ref, lam_ref, yr_ref, q_ref, k_ref, v_ref, hl_ref, tail_ref, *, DB, Ts):
    xn = _rms(x_ref[...], nw_ref[...])
    proj = jnp.dot(xn.astype(BF16), win_ref[...], preferred_element_type=F32)
    xr = proj[:, 0:DR]
    gr = proj[:, DR:2 * DR]
    q_ref[...] = proj[:, 2 * DR:2 * DR + SBW]
    k_ref[...] = proj[:, 2 * DR + SBW:2 * DR + 2 * SBW]
    v_ref[...] = proj[:, 2 * DR + 2 * SBW:2 * DR + 3 * SBW]
    xp = [cbuf_ref[0], cbuf_ref[1], cbuf_ref[2]] + [xr[t * DB:(t + 1) * DB] for t in range(Ts)]
    cw = cw_ref[...]
    xc = jnp.concatenate(
        [cw[0:1] * xp[t] + cw[1:2] * xp[t + 1] + cw[2:3] * xp[t + 2] + cw[3:4] * xp[t + 3] + cb_ref[...]
         for t in range(Ts)], axis=0)
    a, gi, mult = _lru_gates(xc, wa_ref[...], ba_ref[...], wx_ref[...], bx_ref[...], lam_ref[...])
    b = xc * gi * mult
    h = h0_ref[...]
    hs = []
    for t in range(Ts):
        h = a[t * DB:(t + 1) * DB] * h + b[t * DB:(t + 1) * DB]
        hs.append(h)
    yr_ref[...] = jnp.concatenate(hs, axis=0) * jax.nn.gelu(gr)
    hl_ref[...] = h
    for r in range(3):
        tail_ref[r] = xp[Ts + r]


def _even_in_sample(x, nw, win, cbuf, h0, cw, cb, wa, ba, wx, bx, lam, *, DB, Ts):
    N = DB * Ts
    return pl.pallas_call(
        functools.partial(_even_in_sample_kernel, DB=DB, Ts=Ts),
        out_shape=[jax.ShapeDtypeStruct((N, DR), F32), jax.ShapeDtypeStruct((N, SBW), F32),
                   jax.ShapeDtypeStruct((N, SBW), F32), jax.ShapeDtypeStruct((N, SBW), F32),
                   jax.ShapeDtypeStruct((DB, DR), F32), jax.ShapeDtypeStruct((3, DB, DR), F32)],
        compiler_params=pltpu.CompilerParams(vmem_limit_bytes=VMEM_LIMIT),
        name="even_in_sample",
    )(x, nw, win, cbuf, h0, cw, cb, wa, ba, wx, bx, lam)


def _attn_prompt_kernel(qi_ref, kb_ref, q_ref, k_ref, v_ref, bias_ref, tri_ref, o_ref, acc, car, *, tq):
    p = pl.program_id(1)
    qi = qi_ref[p]
    kb = kb_ref[p]

    @pl.when(kb == qi)
    def _():
        acc[...] = jnp.zeros(acc.shape, F32)
        car[...] = jnp.zeros(car.shape, F32)

    row = lax.broadcasted_iota(jnp.int32, (tq, tq), 0)
    col = lax.broadcasted_iota(jnp.int32, (tq, tq), 1)
    valid = (kb * tq + col) < (qi * tq + row)
    lane = lax.broadcasted_iota(jnp.int32, (tq, LANES), 1)
    tri = tri_ref[...]
    for h in range(NH):
        pr, half = h // 2, h % 2
        sl = slice(pr * LANES, (pr + 1) * LANES)
        own = (lane >= HD) if half == 1 else (lane < HD)
        qh = jnp.where(own, q_ref[:, sl], 0.0) * (HD ** -0.5)
        z = _dot_nt(qh, k_ref[:, sl]) + bias_ref[h:h + 1, :]
        log_beta, log_keep = _sb_terms(z)
        lk = jnp.where(valid, log_keep, 0.0)
        later = _dot2(lk, tri)
        tot = log_beta + later + car[h]
        a = jnp.where(valid, jnp.exp(tot), 0.0)
        acc[h] += _dot(a, v_ref[:, sl])
        car[h] += later[:, 0:1] + lk[:, 0:1]

    @pl.when(kb == 0)
    def _():
        for pr in range(NH // 2):
            o_ref[:, pr * LANES:(pr + 1) * LANES] = jnp.where(lane < HD, acc[2 * pr], acc[2 * pr + 1])


def _attn_prompt(q, k, v, bias, *, B, T, tq):
    nq = T // tq
    qi_tab, kb_tab = [], []
    for qi in range(nq):
        for kb in range(qi, -1, -1):
            qi_tab.append(qi)
            kb_tab.append(kb)
    npairs = len(qi_tab)
    qi_tab = jnp.asarray(np.array(qi_tab, np.int32))
    kb_tab = jnp.asarray(np.array(kb_tab, np.int32))
    bias_rows = jnp.broadcast_to(bias.astype(F32)[:, None], (NH, tq))
    tri = (np.arange(tq)[:, None] > np.arange(tq)[None, :]).astype(np.float32)
    tri = jnp.asarray(tri, BF16)
    grid_spec = pltpu.PrefetchScalarGridSpec(
        num_scalar_prefetch=2,
        grid=(B, npairs),
        in_specs=[pl.BlockSpec((tq, SBW), lambda b, p, qt, kt: (b * nq + qt[p], 0)),
                  pl.BlockSpec((tq, SBW), lambda b, p, qt, kt: (b * nq + kt[p], 0)),
                  pl.BlockSpec((tq, SBW), lambda b, p, qt, kt: (b * nq + kt[p], 0)),
                  pl.BlockSpec((NH, tq), lambda b, p, qt, kt: (0, 0)),
                  pl.BlockSpec((tq, tq), lambda b, p, qt, kt: (0, 0))],
        out_specs=pl.BlockSpec((tq, SBW), lambda b, p, qt, kt: (b * nq + qt[p], 0)),
        scratch_shapes=[pltpu.VMEM((NH, tq, LANES), F32), pltpu.VMEM((NH, tq, 1), F32)],
    )
    return pl.pallas_call(
        functools.partial(_attn_prompt_kernel, tq=tq),
        grid_spec=grid_spec,
        out_shape=jax.ShapeDtypeStruct((B * T, SBW), F32),
        compiler_params=_cp("arbitrary", "arbitrary"),
        name="attn_prompt",
    )(qi_tab, kb_tab, q, k, v, bias_rows, tri)


def _attn_sample_kernel(pt_ref, qrep_ref, kn_ref, vn_ref, bias_ref, tri_ref, *refs, Ts, pps, page):
    kp_refs = refs[0:pps]
    vp_refs = refs[pps:2 * pps]
    o_ref = refs[2 * pps]
    acc, car = refs[2 * pps + 1], refs[2 * pps + 2]
    s = pl.program_id(1)
    R = Ts * NH
    rowh = lax.broadcasted_iota(jnp.int32, (R, SBW), 0)
    lanec = lax.broadcasted_iota(jnp.int32, (R, SBW), 1)
    own = (lanec // HD) == (rowh % NH)
    qe = jnp.where(own, qrep_ref[0], 0.0) * (HD ** -0.5)
    bias = bias_ref[...]

    @pl.when(s == 0)
    def _():
        tq = lax.broadcasted_iota(jnp.int32, (R, 1), 0) // NH
        lbs, lks = [], []
        for j in range(Ts):
            zj = jnp.sum(qe * kn_ref[0, j:j + 1, :], axis=-1, keepdims=True) + bias[:, 0:1]
            lb, lkeep = _sb_terms(zj)
            lbs.append(lb)
            lks.append(jnp.where(tq > j, lkeep, 0.0))
        accv = jnp.zeros((R, SBW), F32)
        later = jnp.zeros((R, 1), F32)
        for j in range(Ts - 1, -1, -1):
            aj = jnp.where(tq > j, jnp.exp(lbs[j] + later), 0.0)
            accv = accv + aj * vn_ref[0, j:j + 1, :]
            later = later + lks[j]
        acc[...] = accv
        car[...] = later

    tri = tri_ref[...]
    for r in range(pps):
        z = _dot_nt(qe, kp_refs[r][0]) + bias
        log_beta, log_keep = _sb_terms(z)
        later = _dot2(log_keep, tri)
        a = jnp.exp(log_beta + later + car[...])
        acc[...] += _dot(a, vp_refs[r][0])
        car[...] += later[:, 0:1] + log_keep[:, 0:1]

    @pl.when(s == pl.num_programs(1) - 1)
    def _():
        o_ref[0] = jnp.sum(jnp.where(own, acc[...], 0.0).reshape(Ts, NH, SBW), axis=1)


def _attn_sample(q_seq, k_seq, v_seq, bias, pool_k, pool_v, page_table, *, layer, n_pool, DB, Ts, pps):
    n_pages = page_table.shape[1]
    page = pool_k.shape[1]
    R = Ts * NH
    nsteps = n_pages // pps
    qrep = jnp.broadcast_to(q_seq[:, :, None, :], (DB, Ts, NH, SBW)).reshape(DB, R, SBW)
    bias_rows = jnp.broadcast_to(jnp.tile(bias.astype(F32), Ts)[:, None], (R, page))
    tri = jnp.asarray((np.arange(page)[:, None] > np.arange(page)[None, :]).astype(np.float32), BF16)
    base = layer * n_pool

    def pool_map(r):
        return lambda b, s, pt: (base + pt[b, n_pages - 1 - (s * pps + r)], 0, 0)

    seq = lambda b, s, pt: (b, 0, 0)
    const = lambda b, s, pt: (0, 0)
    grid_spec = pltpu.PrefetchScalarGridSpec(
        num_scalar_prefetch=1,
        grid=(DB, nsteps),
        in_specs=([pl.BlockSpec((1, R, SBW), seq), pl.BlockSpec((1, Ts, SBW), seq), pl.BlockSpec((1, Ts, SBW), seq),
                   pl.BlockSpec((R, page), const), pl.BlockSpec((page, page), const)]
                  + [pl.BlockSpec((1, page, SBW), pool_map(r)) for r in range(pps)]
                  + [pl.BlockSpec((1, page, SBW), pool_map(r)) for r in range(pps)]),
        out_specs=pl.BlockSpec((1, Ts, SBW), seq),
        scratch_shapes=[pltpu.VMEM((R, SBW), F32), pltpu.VMEM((R, 1), F32)],
    )
    return pl.pallas_call(
        functools.partial(_attn_sample_kernel, Ts=Ts, pps=pps, page=page),
        grid_spec=grid_spec,
        out_shape=jax.ShapeDtypeStruct((DB, Ts, SBW), F32),
        compiler_params=_cp("arbitrary", "arbitrary"),
        name="attn_sample",
    )(page_table, qrep, k_seq, v_seq, bias_rows, tri, *([pool_k] * pps), *([pool_v] * pps))


def _mix_out_kernel(*refs, nparts):
    x_ref = refs[0]
    parts = refs[1:1 + nparts]
    w_ref, nw_ref, o_ref = refs[1 + nparts], refs[2 + nparts], refs[3 + nparts]
    off = 0
    y = None
    for pr in parts:
        kdim = pr.shape[1]
        t = jnp.dot(pr[...].astype(BF16), w_ref[off:off + kdim, :], preferred_element_type=F32)
        y = t if y is None else y + t
        off += kdim
    o_ref[...] = x_ref[...] + _rms(y, nw_ref[...])


def _mix_out(x, parts, w, nw, *, tm):
    N = x.shape[0]
    row = lambda i: (i, 0)
    const = lambda i: (0, 0)
    return pl.pallas_call(
        functools.partial(_mix_out_kernel, nparts=len(parts)),
        grid=(N // tm,),
        in_specs=([pl.BlockSpec((tm, D), row)] + [pl.BlockSpec((tm, p.shape[1]), row) for p in parts]
                  + [pl.BlockSpec(w.shape, const), pl.BlockSpec((1, D), const)]),
        out_specs=pl.BlockSpec((tm, D), row),
        out_shape=jax.ShapeDtypeStruct((N, D), F32),
        compiler_params=_cp("arbitrary"),
        name="mix_out",
    )(x, *parts, w, nw)


def _ffn_prompt_kernel(x_ref, nwa_ref, nwb_ref, wg_ref, wv_ref, cwg_ref, cwv_ref, cbg_ref, cbv_ref, wd_ref,
                       o_ref, tg_ref, tv_ref, xn, acc, halo_g, halo_v, ubuf, *, tm, tiles_per_seq):
    i = pl.program_id(0)
    k = pl.program_id(1)

    @pl.when(k == 0)
    def _():
        xn[...] = _rms(x_ref[...], nwa_ref[...]).astype(BF16)
        acc[...] = jnp.zeros(acc.shape, F32)

    seq_start = (i % tiles_per_seq) == 0

    def conv(w_ref, cw_ref, cb_ref, halo, t_ref):
        u = jnp.dot(xn[...], w_ref[...], preferred_element_type=F32)
        prev = halo[k]
        ubuf[0:8, :] = jnp.where(seq_start, 0.0, prev)
        ubuf[8:tm + 8, :] = u
        halo[k] = u[tm - 8:tm, :]
        t_ref[0] = u[tm - 2:tm, :]
        cw = cw_ref[...]
        return cw[2:3] * u + cw[1:2] * ubuf[pl.ds(7, tm), :] + cw[0:1] * ubuf[pl.ds(6, tm), :] + cb_ref[...]

    cg = conv(wg_ref, cwg_ref, cbg_ref, halo_g, tg_ref)
    cv = conv(wv_ref, cwv_ref, cbv_ref, halo_v, tv_ref)
    hmid = jax.nn.gelu(cg) * cv
    acc[...] += jnp.dot(hmid.astype(BF16), wd_ref[...], preferred_element_type=F32)

    @pl.when(k == pl.num_programs(1) - 1)
    def _():
        o_ref[...] = x_ref[...] + _rms(acc[...], nwb_ref[...])


def _ffn_prompt(x, nwa, nwb, wup, cw, cb, wdn, *, T, tm, tf):
    N = x.shape[0]
    nk = DFF // tf
    ntiles = N // tm
    row = lambda i, k: (i, 0)
    const = lambda i, k: (0, 0)
    return pl.pallas_call(
        functools.partial(_ffn_prompt_kernel, tm=tm, tiles_per_seq=T // tm),
        grid=(ntiles, nk),
        in_specs=[pl.BlockSpec((tm, D), row), pl.BlockSpec((1, D), const), pl.BlockSpec((1, D), const),
                  pl.BlockSpec((D, tf), lambda i, k: (0, k)), pl.BlockSpec((D, tf), lambda i, k: (0, nk + k)),
                  pl.BlockSpec((3, tf), lambda i, k: (0, k)), pl.BlockSpec((3, tf), lambda i, k: (0, nk + k)),
                  pl.BlockSpec((1, tf), lambda i, k: (0, k)), pl.BlockSpec((1, tf), lambda i, k: (0, nk + k)),
                  pl.BlockSpec((tf, D), lambda i, k: (k, 0))],
        out_specs=[pl.BlockSpec((tm, D), row), pl.BlockSpec((1, 2, tf), lambda i, k: (i, 0, k)),
                   pl.BlockSpec((1, 2, tf), lambda i, k: (i, 0, k))],
        out_shape=[jax.ShapeDtypeStruct((N, D), F32), jax.ShapeDtypeStruct((ntiles, 2, DFF), F32),
                   jax.ShapeDtypeStruct((ntiles, 2, DFF), F32)],
        scratch_shapes=[pltpu.VMEM((tm, D), BF16), pltpu.VMEM((tm, D), F32), pltpu.VMEM((nk, 8, tf), F32),
                        pltpu.VMEM((nk, 8, tf), F32), pltpu.VMEM((tm + 8, tf), F32)],
        compiler_params=_cp("arbitrary", "arbitrary"),
        name="ffn_prompt",
    )(x, nwa, nwb, wup, wup, cw, cw, cb, cb, wdn)


def _ffn_sample_kernel(x_ref, nwa_ref, nwb_ref, wg_ref, wv_ref, cwg_ref, cwv_ref, cbg_ref, cbv_ref, wd_ref,
                       b0g_ref, b0v_ref, b1g_ref, b1v_ref, o_ref, n0g_ref, n0v_ref, n1g_ref, n1v_ref,
                       xn, acc, *, DB, Ts):
    k = pl.program_id(0)

    @pl.when(k == 0)
    def _():
        xn[...] = _rms(x_ref[...], nwa_ref[...]).astype(BF16)
        acc[...] = jnp.zeros(acc.shape, F32)

    def conv(w_ref, cw_ref, cb_ref, b0_ref, b1_ref, n0_ref, n1_ref):
        u = jnp.dot(xn[...], w_ref[...], preferred_element_type=F32)
        xp = [b0_ref[...], b1_ref[...]] + [u[t * DB:(t + 1) * DB] for t in range(Ts)]
        n0_ref[...] = xp[Ts]
        n1_ref[...] = xp[Ts + 1]
        cw = cw_ref[...]
        return jnp.concatenate(
            [cw[0:1] * xp[t] + cw[1:2] * xp[t + 1] + cw[2:3] * xp[t + 2] + cb_ref[...] for t in range(Ts)], axis=0)

    cg = conv(wg_ref, cwg_ref, cbg_ref, b0g_ref, b1g_ref, n0g_ref, n1g_ref)
    cv = conv(wv_ref, cwv_ref, cbv_ref, b0v_ref, b1v_ref, n0v_ref, n1v_ref)
    hmid = jax.nn.gelu(cg) * cv
    acc[...] += jnp.dot(hmid.astype(BF16), wd_ref[...], preferred_element_type=F32)

    @pl.when(k == pl.num_programs(0) - 1)
    def _():
        o_ref[...] = x_ref[...] + _rms(acc[...], nwb_ref[...])


def _ffn_sample(x, nwa, nwb, wup, cw, cb, wdn, fbuf, *, layer, DB, Ts, tf):
    N = DB * Ts
    nk = DFF // tf
    const = lambda k: (0, 0)
    piece = jax.ShapeDtypeStruct((DB, DFF), F32)
    return pl.pallas_call(
        functools.partial(_ffn_sample_kernel, DB=DB, Ts=Ts),
        grid=(nk,),
        in_specs=[pl.BlockSpec((N, D), const), pl.BlockSpec((1, D), const), pl.BlockSpec((1, D), const),
                  pl.BlockSpec((D, tf), lambda k: (0, k)), pl.BlockSpec((D, tf), lambda k: (0, nk + k)),
                  pl.BlockSpec((3, tf), lambda k: (0, k)), pl.BlockSpec((3, tf), lambda k: (0, nk + k)),
                  pl.BlockSpec((1, tf), lambda k: (0, k)), pl.BlockSpec((1, tf), lambda k: (0, nk + k)),
                  pl.BlockSpec((tf, D), lambda k: (k, 0)),
                  pl.BlockSpec((DB, tf), lambda k: (layer, k)), pl.BlockSpec((DB, tf), lambda k: (layer, nk + k)),
                  pl.BlockSpec((DB, tf), lambda k: (layer, 2 * nk + k)),
                  pl.BlockSpec((DB, tf), lambda k: (layer, 3 * nk + k))],
        out_specs=[pl.BlockSpec((N, D), const)] + [pl.BlockSpec((DB, tf), lambda k: (0, k))] * 4,
        out_shape=[jax.ShapeDtypeStruct((N, D), F32), piece, piece, piece, piece],
        scratch_shapes=[pltpu.VMEM((N, D), BF16), pltpu.VMEM((N, D), F32)],
        compiler_params=_cp("arbitrary"),
        name="ffn_sample",
    )(x, nwa, nwb, wup, wup, cw, cw, cb, cb, wdn, fbuf, fbuf, fbuf, fbuf)


def _odd_post(c, ba, alog, dtb, g_ref, gt_ref):
    c = c * jax.nn.sigmoid(c)
    q = c[:, 0:KW]
    k = c[:, KW:2 * KW]
    v = c[:, 2 * KW:CCH]
    G = g_ref[...]
    GT = gt_ref[...]
    q = q * _dot2(lax.rsqrt(_dot2(q * q, G) + EPS), GT)
    k = k * _dot2(lax.rsqrt(_dot2(k * k, G) + EPS), GT)
    lane = lax.broadcasted_iota(jnp.int32, ba.shape, 1)
    beta = jax.nn.sigmoid(ba)
    g = -jnp.exp(alog) * jax.nn.softplus(ba + dtb)
    gates = jnp.where(lane < NH, beta, g)
    return q, k, v, gates


def _odd_in_prompt_kernel(x_ref, nw_ref, win_ref, cw_ref, alog_ref, dtb_ref, g_ref, gt_ref,
                          q_ref, k_ref, v_ref, z_ref, gates_ref, tail_ref, cbuf, *, tm):
    i = pl.program_id(1)
    xn = _rms(x_ref[...], nw_ref[...])
    proj = jnp.dot(xn.astype(BF16), win_ref[...], preferred_element_type=F32)
    qkv = proj[:, 0:CCH]
    z_ref[...] = proj[:, CCH:CCH + VW]
    ba = proj[:, CCH + VW:O_PAD]

    @pl.when(i == 0)
    def _():
        cbuf[0:8, :] = jnp.zeros((8, CCH), F32)

    @pl.when(i > 0)
    def _():
        cbuf[0:8, :] = cbuf[tm:tm + 8, :]

    cbuf[8:tm + 8, :] = qkv
    cw = cw_ref[...]
    c = (cw[3:4] * qkv + cw[2:3] * cbuf[pl.ds(7, tm), :] + cw[1:2] * cbuf[pl.ds(6, tm), :]
         + cw[0:1] * cbuf[pl.ds(5, tm), :])
    q, k, v, gates = _odd_post(c, ba, alog_ref[...], dtb_ref[...], g_ref, gt_ref)
    q_ref[...] = q
    k_ref[...] = k
    v_ref[...] = v
    gates_ref[...] = gates
    tail_ref[0] = cbuf[pl.ds(tm + 5, 3), :]


def _odd_in_prompt(x, nw, win, cw, alog, dtb, G, GT, *, B, T, tm):
    nt = T // tm
    N = B * T
    row = lambda b, i: (b * nt + i, 0)
    const = lambda b, i: (0, 0)
    return pl.pallas_call(
        functools.partial(_odd_in_prompt_kernel, tm=tm),
        grid=(B, nt),
        in_specs=[pl.BlockSpec((tm, D), row), pl.BlockSpec((1, D), const), pl.BlockSpec((D, O_PAD), const),
                  pl.BlockSpec((4, CCH), const), pl.BlockSpec((1, LANES), const), pl.BlockSpec((1, LANES), const),
                  pl.BlockSpec((KW, LANES), const), pl.BlockSpec((LANES, KW), const)],
        out_specs=[pl.BlockSpec((tm, KW), row), pl.BlockSpec((tm, KW), row), pl.BlockSpec((tm, VW), row),
                   pl.BlockSpec((tm, VW), row), pl.BlockSpec((tm, LANES), row),
                   pl.BlockSpec((1, 3, CCH), lambda b, i: (b, 0, 0))],
        out_shape=[jax.ShapeDtypeStruct((N, KW), F32), jax.ShapeDtypeStruct((N, KW), F32),
                   jax.ShapeDtypeStruct((N, VW), F32), jax.ShapeDtypeStruct((N, VW), F32),
                   jax.ShapeDtypeStruct((N, LANES), F32), jax.ShapeDtypeStruct((B, 3, CCH), F32)],
        scratch_shapes=[pltpu.VMEM((tm + 8, CCH), F32)],
        compiler_params=_cp("arbitrary", "arbitrary"),
        name="odd_in_prompt",
    )(x, nw, win, cw, alog, dtb, G, GT)


def _odd_in_sample_kernel(x_ref, nw_ref, win_ref, cbuf_ref, cw_ref, alog_ref, dtb_ref, g_ref, gt_ref,
                          q_ref, k_ref, v_ref, z_ref, gates_ref, tail_ref, *, DB, Ts):
    xn = _rms(x_ref[...], nw_ref[...])
    proj = jnp.dot(xn.astype(BF16), win_ref[...], preferred_element_type=F32)
    qkv = proj[:, 0:CCH]
    z_ref[...] = proj[:, CCH:CCH + VW]
    ba = proj[:, CCH + VW:O_PAD]
    xp = [cbuf_ref[0], cbuf_ref[1], cbuf_ref[2]] + [qkv[t * DB:(t + 1) * DB] for t in range(Ts)]
    cw = cw_ref[...]
    c = jnp.concatenate(
        [cw[0:1] * xp[t] + cw[1:2] * xp[t + 1] + cw[2:3] * xp[t + 2] + cw[3:4] * xp[t + 3] for t in range(Ts)],
        axis=0)
    q, k, v, gates = _odd_post(c, ba, alog_ref[...], dtb_ref[...], g_ref, gt_ref)
    q_ref[...] = q
    k_ref[...] = k
    v_ref[...] = v
    gates_ref[...] = gates
    for r in range(3):
        tail_ref[r] = xp[Ts + r]


def _odd_in_sample(x, nw, win, cbuf, cw, alog, dtb, G, GT, *, DB, Ts):
    N = DB * Ts
    return pl.pallas_call(
        functools.partial(_odd_in_sample_kernel, DB=DB, Ts=Ts),
        out_shape=[jax.ShapeDtypeStruct((N, KW), F32), jax.ShapeDtypeStruct((N, KW), F32),
                   jax.ShapeDtypeStruct((N, VW), F32), jax.ShapeDtypeStruct((N, VW), F32),
                   jax.ShapeDtypeStruct((N, LANES), F32), jax.ShapeDtypeStruct((3, DB, CCH), F32)],
        compiler_params=pltpu.CompilerParams(vmem_limit_bytes=VMEM_LIMIT),
        name="odd_in_sample",
    )(x, nw, win, cbuf, cw, alog, dtb, G, GT)


def _delta_kernel(q_ref, k_ref, v_ref, z_ref, gates_ref, s0_ref, nw_ref, o_ref, sout_ref, S, *, C):
    c_idx = pl.program_id(1)

    @pl.when(c_idx == 0)
    def _():
        S[...] = s0_ref[0]

    gates = gates_ref[...]
    gcs = _cumsum_rows(gates)
    gpad = jnp.concatenate([gcs, jnp.zeros((LANES - C, LANES), F32)], axis=0) if C < LANES else gcs
    gcs_t = gpad.T
    row = lax.broadcasted_iota(jnp.int32, (C, C), 0)
    col = lax.broadcasted_iota(jnp.int32, (C, C), 1)
    incl = row >= col
    strict = row > col
    eye = (row == col).astype(F32)
    nw = nw_ref[...]
    for h in range(NH):
        qh = q_ref[:, h * DK:(h + 1) * DK] * (DK ** -0.5)
        kh = k_ref[:, h * DK:(h + 1) * DK]
        vh = v_ref[:, h * DV:(h + 1) * DV]
        beta = gates[:, h:h + 1]
        gc = gcs[:, NH + h:NH + h + 1]
        gr = gcs_t[NH + h:NH + h + 1, 0:C]
        decay = jnp.where(incl, jnp.exp(jnp.where(incl, gc - gr, 0.0)), 0.0)
        kb = kh * beta
        lm = jnp.where(strict, _dot_nt(kb, kh) * decay, 0.0)
        rhs = jnp.concatenate([vh * beta, kb * jnp.exp(gc)], axis=-1)
        pw = -lm
        sol = rhs + _dot3(pw, rhs)
        n = 2
        while n < C:
            pw = _dot3(pw, pw)
            sol = sol + _dot3(pw, sol)
            n *= 2
        u = sol[:, 0:DV]
        w = sol[:, DV:DV + DK]
        Sh = S[h]
        v_new = u - _dot(w, Sh)
        attn = _dot_nt(qh, kh) * decay
        o = _dot(qh * jnp.exp(gc), Sh) + _dot(attn, v_new)
        g_last = gc[C - 1:C, :]
        S[h] = Sh * jnp.exp(g_last) + _dot_tn(kh * jnp.exp(g_last - gc), v_new)
        zh = z_ref[:, h * DV:(h + 1) * DV]
        o_ref[:, h * DV:(h + 1) * DV] = _rms(o, nw) * (zh * jax.nn.sigmoid(zh))

    @pl.when(c_idx == pl.num_programs(1) - 1)
    def _():
        sout_ref[0] = S[...]


def _delta(q, k, v, z, gates, s0, nw, *, nseq, nchunks, C, s_base):
    N = nseq * nchunks * C
    row = lambda b, c: (b * nchunks + c, 0)
    return pl.pallas_call(
        functools.partial(_delta_kernel, C=C),
        grid=(nseq, nchunks),
        in_specs=[pl.BlockSpec((C, KW), row), pl.BlockSpec((C, KW), row), pl.BlockSpec((C, VW), row),
                  pl.BlockSpec((C, VW), row), pl.BlockSpec((C, LANES), row),
                  pl.BlockSpec((1, NH, DK, DV), lambda b, c: (s_base + b, 0, 0, 0)),
                  pl.BlockSpec((1, DV), lambda b, c: (0, 0))],
        out_specs=[pl.BlockSpec((C, VW), row), pl.BlockSpec((1, NH, DK, DV), lambda b, c: (b, 0, 0, 0))],
        out_shape=[jax.ShapeDtypeStruct((N, VW), F32), jax.ShapeDtypeStruct((nseq, NH, DK, DV), F32)],
        scratch_shapes=[pltpu.VMEM((NH, DK, DV), F32)],
        compiler_params=_cp("arbitrary", "arbitrary"),
        name="delta",
    )(q, k, v, z, gates, s0, nw)


def _tmajor_to_seq(a, DB, Ts):
    return a.reshape(Ts, DB, a.shape[-1]).transpose(1, 0, 2)


def _pad_seq(a, DB, Ts, Tp):
    s = _tmajor_to_seq(a, DB, Ts)
    s = jnp.pad(s, ((0, 0), (0, Tp - Ts), (0, 0)))
    return s.reshape(DB * Tp, a.shape[-1])


def kernel(x_prompt, x_sample, cache_sb_k, cache_sb_v, state_lru_h, state_lru_conv, state_dn_S, state_dn_conv,
           state_ffn_conv, page_table, norm_mix_pre, norm_mix_post, norm_ffn_pre, norm_ffn_post, w_in_e,
           lru_conv_w, lru_conv_b, lru_wa, lru_ba, lru_wx, lru_bx, lru_lambda, sb_bias, w_out_e, w_in_o,
           dn_conv_w, dn_A_log, dn_dt_bias, dn_norm_w, w_out_o, ffn_w_up, ffn_conv_w, ffn_conv_b, ffn_w_down):
    B, T, d_model = x_prompt.shape
    DB, Ts, _ = x_sample.shape
    depth = norm_mix_pre.shape[0]
    n_even, n_pool, page = cache_sb_k.shape[0], cache_sb_k.shape[1], cache_sb_k.shape[2]
    n_pages = page_table.shape[1]
    assert d_model == D and w_in_e.shape[-1] == E_IN and w_in_o.shape[-1] == O_IN
    assert ffn_w_down.shape[1] == DFF and n_pages * page > 0 and Ts >= 3
    tm = min(512, T)
    tq = min(256, T)
    tf = 512
    C = min(DN_CHUNK, T)
    Cs = 8
    pps = 4 if n_pages % 4 == 0 else 1
    assert T % tm == 0 and T % tq == 0 and T % C == 0 and Ts <= Cs and (DB * Ts) % 8 == 0

    xp = x_prompt.reshape(B * T, D)
    xs = x_sample.transpose(1, 0, 2).reshape(Ts * DB, D)
    pool_k = cache_sb_k.reshape(n_even * n_pool, page, SBW)
    pool_v = cache_sb_v.reshape(n_even * n_pool, page, SBW)
    dn_S_all = state_dn_S.reshape(-1, NH, DK, DV)
    fbuf_all = state_ffn_conv.reshape(depth * DB, 2 * 2 * DFF)
    s0_zero = jnp.zeros((B, NH, DK, DV), F32)

    G = jnp.asarray((np.arange(KW)[:, None] // DK == np.arange(LANES)[None, :]).astype(np.float32), BF16)
    GT = G.T

    r1 = lambda a: a.reshape(1, -1)
    ks, vs, hs, lcs, Ss, dcs, fcs = ([[], []] for _ in range(7))
    for l in range(depth):
        j = l // 2
        nw_pre = r1(norm_mix_pre[l])
        if l % 2 == 0:
            win = w_in_e[j].astype(BF16)
            wa = jax.scipy.linalg.block_diag(*lru_wa[j]).astype(BF16)
            wx = jax.scipy.linalg.block_diag(*lru_wx[j]).astype(BF16)
            pe = (lru_conv_w[j], r1(lru_conv_b[j]), wa, r1(lru_ba[j]), wx, r1(lru_bx[j]), r1(lru_lambda[j]))
            yr_p, q_p, k_p, v_p, hl_p, lc_p = _even_in_prompt(xp, nw_pre, win, *pe, B=B, T=T, tm=tm)
            ya_p = _attn_prompt(q_p, k_p, v_p, sb_bias[j], B=B, T=T, tq=tq)
            yr_s, q_s, k_s, v_s, hl_s, lc_s = _even_in_sample(
                xs, nw_pre, win, state_lru_conv[j].transpose(1, 0, 2), state_lru_h[j], *pe, DB=DB, Ts=Ts)
            q_s, k_s, v_s = (_tmajor_to_seq(a, DB, Ts) for a in (q_s, k_s, v_s))
            ya_s = _attn_sample(q_s, k_s, v_s, sb_bias[j], pool_k, pool_v, page_table, layer=j, n_pool=n_pool,
                                DB=DB, Ts=Ts, pps=pps)
            ya_s = ya_s.transpose(1, 0, 2).reshape(Ts * DB, SBW)
            wout = w_out_e[j].astype(BF16)
            nw_post = r1(norm_mix_post[l])
            xp = _mix_out(xp, [yr_p, ya_p], wout, nw_post, tm=tm)
            xs = _mix_out(xs, [yr_s, ya_s], wout, nw_post, tm=min(512, Ts * DB))
            ks[0].append(k_p.reshape(B, T, NH, HD)); ks[1].append(k_s.reshape(DB, Ts, NH, HD))
            vs[0].append(v_p.reshape(B, T, NH, HD)); vs[1].append(v_s.reshape(DB, Ts, NH, HD))
            hs[0].append(hl_p.reshape(B, DR)); hs[1].append(hl_s)
            lcs[0].append(lc_p); lcs[1].append(lc_s.transpose(1, 0, 2))
        else:
            win = jnp.pad(w_in_o[j], ((0, 0), (0, O_PAD - O_IN))).astype(BF16)
            alog = jnp.zeros((1, LANES), F32).at[0, NH:2 * NH].set(dn_A_log[j])
            dtb = jnp.zeros((1, LANES), F32).at[0, NH:2 * NH].set(dn_dt_bias[j])
            nwd = r1(dn_norm_w[j])
            q_p, k_p, v_p, z_p, g_p, dc_p = _odd_in_prompt(xp, nw_pre, win, dn_conv_w[j], alog, dtb, G, GT,
                                                           B=B, T=T, tm=tm)
            o_p, S_p = _delta(q_p, k_p, v_p, z_p, g_p, s0_zero, nwd, nseq=B, nchunks=T // C, C=C, s_base=0)
            q_s, k_s, v_s, z_s, g_s, dc_s = _odd_in_sample(
                xs, nw_pre, win, state_dn_conv[j].transpose(1, 0, 2), dn_conv_w[j], alog, dtb, G, GT, DB=DB, Ts=Ts)
            q_s, k_s, v_s, z_s, g_s = (_pad_seq(a, DB, Ts, Cs) for a in (q_s, k_s, v_s, z_s, g_s))
            o_s, S_s = _delta(q_s, k_s, v_s, z_s, g_s, dn_S_all, nwd, nseq=DB, nchunks=1, C=Cs, s_base=j * DB)
            o_s = o_s.reshape(DB, Cs, VW)[:, :Ts].transpose(1, 0, 2).reshape(Ts * DB, VW)
            wout = w_out_o[j].astype(BF16)
            nw_post = r1(norm_mix_post[l])
            xp = _mix_out(xp, [o_p], wout, nw_post, tm=tm)
            xs = _mix_out(xs, [o_s], wout, nw_post, tm=min(512, Ts * DB))
            Ss[0].append(S_p); Ss[1].append(S_s)
            dcs[0].append(dc_p); dcs[1].append(dc_s.transpose(1, 0, 2))
        wup = ffn_w_up[l].astype(BF16)
        wdn = ffn_w_down[l].astype(BF16)
        fa = (r1(norm_ffn_pre[l]), r1(norm_ffn_post[l]), wup, ffn_conv_w[l], r1(ffn_conv_b[l]), wdn)
        xp, tg, tv = _ffn_prompt(xp, *fa, T=T, tm=tm, tf=tf)
        last = (np.arange(B) + 1) * (T // tm) - 1
        fcs[0].append(jnp.concatenate([tg[last], tv[last]], axis=-1))
        xs, n0g, n0v, n1g, n1v = _ffn_sample(xs, *fa, fbuf_all, layer=l, DB=DB, Ts=Ts, tf=tf)
        fcs[1].append(jnp.stack([jnp.concatenate([n0g, n0v], axis=-1), jnp.concatenate([n1g, n1v], axis=-1)], axis=1))

    y_prompt = xp.reshape(B, T, D)
    y_sample = xs.reshape(Ts, DB, D).transpose(1, 0, 2)
    st = jnp.stack
    return (y_prompt, y_sample, st(ks[0]), st(vs[0]), st(ks[1]), st(vs[1]), st(hs[0]), st(hs[1]),
            st(lcs[0]), st(lcs[1]), st(Ss[0]), st(Ss[1]), st(dcs[0]), st(dcs[1]), st(fcs[0]), st(fcs[1]))
```

```python
import functools

import numpy as np
import jax
import jax.numpy as jnp
from jax import lax
from jax.experimental import pallas as pl
from jax.experimental.pallas import tpu as pltpu

F32 = jnp.float32
BF16 = jnp.bfloat16
EPS = 1e-6

D = 1024
DR = 512
SBW = 512
NH = 8
HD = 64
LRU_C = 8.0
E_IN = 2 * DR + 3 * SBW
DK = 64
DV = 128
KW = NH * DK
VW = NH * DV
CCH = 2 * KW + VW
O_IN = CCH + VW + 2 * NH
O_PAD = CCH + VW + 128
DFF = 4096
FF_SUB = 512
DN_CHUNK = 64
LANES = 128
LOG2E = 1.4426950408889634
SIGN_BIT = np.int32(-2 ** 31)
MASKED_Z = -1e30
VMEM_LIMIT = 52 * 1024 * 1024


def _cp(*sem):
    return pltpu.CompilerParams(dimension_semantics=sem, vmem_limit_bytes=VMEM_LIMIT)


def _rms(x, w):
    ms = jnp.mean(x * x, axis=-1, keepdims=True)
    return x * lax.rsqrt(ms + EPS) * w


def _dot(a, b):
    return jnp.dot(a.astype(BF16), b.astype(BF16), preferred_element_type=F32)


def _dot_nt(a, b):
    return lax.dot_general(a.astype(BF16), b.astype(BF16), (((1,), (1,)), ((), ())),
                           preferred_element_type=F32)


def _dot_tn(a, b):
    return lax.dot_general(a.astype(BF16), b.astype(BF16), (((0,), (0,)), ((), ())),
                           preferred_element_type=F32)


def _split(a):
    hi = a.astype(BF16)
    lo = (a - hi.astype(F32)).astype(BF16)
    return hi, lo


def _dot2(a, b_exact):
    hi, lo = _split(a)
    b = b_exact.astype(BF16)
    return (jnp.dot(hi, b, preferred_element_type=F32) + jnp.dot(lo, b, preferred_element_type=F32))


def _dot3(a, b):
    ah, al = _split(a)
    bh, bl = _split(b)
    return (jnp.dot(ah, bh, preferred_element_type=F32) + jnp.dot(ah, bl, preferred_element_type=F32)
            + jnp.dot(al, bh, preferred_element_type=F32))


def _expm1(x):
    return jnp.tanh(0.5 * x) * (jnp.exp(x) + 1.0)


def _scan_rows(a, b):
    n = a.shape[0]
    row = lax.broadcasted_iota(jnp.int32, a.shape, 0)
    s = 1
    while s < n:
        m = row >= s
        a_sh = pltpu.roll(a, s, 0)
        b_sh = pltpu.roll(b, s, 0)
        b = jnp.where(m, a * b_sh + b, b)
        a = jnp.where(m, a * a_sh, a)
        s *= 2
    return a, b


def _cumsum_rows(x):
    n = x.shape[0]
    row = lax.broadcasted_iota(jnp.int32, x.shape, 0)
    s = 1
    while s < n:
        x = x + jnp.where(row >= s, pltpu.roll(x, s, 0), 0.0)
        s *= 2
    return x


def _lru_gates(xc, wa, ba, wx, bx, lam):
    r = jax.nn.sigmoid(_dot(xc, wa) + ba)
    gi = jax.nn.sigmoid(_dot(xc, wx) + bx)
    log_a = -LRU_C * r * jax.nn.softplus(-lam)
    a = jnp.exp(log_a)
    mult = jnp.sqrt(-_expm1(2.0 * log_a))
    return a, gi, mult


def _sb_terms(z):
    l1p = jnp.log1p(jnp.exp(-jnp.abs(z)))
    log_beta = jnp.minimum(z, 0.0) - l1p
    return log_beta, log_beta - z


def _even_in_prompt_kernel(x_ref, nw_ref, win_ref, cw_ref, cb_ref, wa_ref, ba_ref, wx_ref, bx_ref, lam_ref,
                           yr_ref, q_ref, k_ref, v_ref, k16_ref, v16_ref, hl_ref, tail_ref, xbuf, hcar, *, tm):
    i = pl.program_id(1)
    xn = _rms(x_ref[...], nw_ref[...])
    proj = jnp.dot(xn.astype(BF16), win_ref[...], preferred_element_type=F32)
    xr = proj[:, 0:DR]
    gr = proj[:, DR:2 * DR]
    kf = proj[:, 2 * DR + SBW:2 * DR + 2 * SBW]
    vf = proj[:, 2 * DR + 2 * SBW:2 * DR + 3 * SBW]
    q_ref[...] = proj[:, 2 * DR:2 * DR + SBW]
    k_ref[...] = kf
    v_ref[...] = vf
    k16_ref[...] = kf.astype(BF16)
    v16_ref[...] = vf.astype(BF16)

    @pl.when(i == 0)
    def _():
        xbuf[0:8, :] = jnp.zeros((8, DR), F32)
        hcar[...] = jnp.zeros((1, DR), F32)

    @pl.when(i > 0)
    def _():
        xbuf[0:8, :] = xbuf[tm:tm + 8, :]

    xbuf[8:tm + 8, :] = xr
    cw = cw_ref[...]
    xc = (cw[3:4] * xr + cw[2:3] * xbuf[pl.ds(7, tm), :] + cw[1:2] * xbuf[pl.ds(6, tm), :]
          + cw[0:1] * xbuf[pl.ds(5, tm), :] + cb_ref[...])
    a, gi, mult = _lru_gates(xc, wa_ref[...], ba_ref[...], wx_ref[...], bx_ref[...], lam_ref[...])
    row = lax.broadcasted_iota(jnp.int32, (tm, DR), 0)
    mult = jnp.where(jnp.logical_and(row == 0, i == 0), 1.0, mult)
    b = xc * gi * mult
    pa, hb = _scan_rows(a, b)
    h = hb + pa * hcar[...]
    hcar[...] = h[tm - 1:tm, :]
    yr_ref[...] = h * jax.nn.gelu(gr)
    hl_ref[0] = h[tm - 1:tm, :]
    tail_ref[0] = xbuf[pl.ds(tm + 5, 3), :]


def _even_in_prompt(x, nw, win, cw, cb, wa, ba, wx, bx, lam, *, B, T, tm):
    nt = T // tm
    N = B * T
    row = lambda b, i: (b * nt + i, 0)
    const = lambda b, i: (0, 0)
    outs = pl.pallas_call(
        functools.partial(_even_in_prompt_kernel, tm=tm),
        grid=(B, nt),
        in_specs=[pl.BlockSpec((tm, D), row), pl.BlockSpec((1, D), const), pl.BlockSpec((D, E_IN), const),
                  pl.BlockSpec((4, DR), const), pl.BlockSpec((1, DR), const), pl.BlockSpec((DR, DR), const),
                  pl.BlockSpec((1, DR), const), pl.BlockSpec((DR, DR), const), pl.BlockSpec((1, DR), const),
                  pl.BlockSpec((1, DR), const)],
        out_specs=[pl.BlockSpec((tm, DR), row), pl.BlockSpec((tm, SBW), row), pl.BlockSpec((tm, SBW), row),
                   pl.BlockSpec((tm, SBW), row), pl.BlockSpec((tm, SBW), row), pl.BlockSpec((tm, SBW), row),
                   pl.BlockSpec((1, 1, DR), lambda b, i: (b, 0, 0)),
                   pl.BlockSpec((1, 3, DR), lambda b, i: (b, 0, 0))],
        out_shape=[jax.ShapeDtypeStruct((N, DR), F32), jax.ShapeDtypeStruct((N, SBW), F32),
                   jax.ShapeDtypeStruct((N, SBW), F32), jax.ShapeDtypeStruct((N, SBW), F32),
                   jax.ShapeDtypeStruct((N, SBW), BF16), jax.ShapeDtypeStruct((N, SBW), BF16),
                   jax.ShapeDtypeStruct((B, 1, DR), F32), jax.ShapeDtypeStruct((B, 3, DR), F32)],
        scratch_shapes=[pltpu.VMEM((tm + 8, DR), F32), pltpu.VMEM((1, DR), F32)],
        compiler_params=_cp("arbitrary", "arbitrary"),
        name="even_in_prompt",
    )(x, nw, win, cw, cb, wa, ba, wx, bx, lam)
    return outs


def _even_in_sample_kernel(x_ref, nw_ref, win_ref, cbuf_ref, h0_ref, cw_ref, cb_ref, wa_ref, ba_ref, wx_ref,
                           bx_ref, lam_ref, yr_ref, q_ref, k_ref, v_ref, hl_ref, tail_ref, *, DB, Ts):
    xn = _rms(x_ref[...], nw_ref[...])
    proj = jnp.dot(xn.astype(BF16), win_ref[...], preferred_element_type=F32)
    xr = proj[:, 0:DR]
    gr = proj[:, DR:2 * DR]
    q_ref[...] = proj[:, 2 * DR:2 * DR + SBW]
    k_ref[...] = proj[:, 2 * DR + SBW:2 * DR + 2 * SBW]
    v_ref[...] = proj[:, 2 * DR + 2 * SBW:2 * DR + 3 * SBW]
    xp = [cbuf_ref[0], cbuf_ref[1], cbuf_ref[2]] + [xr[t * DB:(t + 1) * DB] for t in range(Ts)]
    cw = cw_ref[...]
    xc = jnp.concatenate(
        [cw[0:1] * xp[t] + cw[1:2] * xp[t + 1] + cw[2:3] * xp[t + 2] + cw[3:4] * xp[t + 3] + cb_ref[...]
         for t in range(Ts)], axis=0)
    a, gi, mult = _lru_gates(xc, wa_ref[...], ba_ref[...], wx_ref[...], bx_ref[...], lam_ref[...])
    b = xc * gi * mult
    h = h0_ref[...]
    hs = []
    for t in range(Ts):
        h = a[t * DB:(t + 1) * DB] * h + b[t * DB:(t + 1) * DB]
        hs.append(h)
    yr_ref[...] = jnp.concatenate(hs, axis=0) * jax.nn.gelu(gr)
    hl_ref[...] = h
    for r in range(3):
        tail_ref[r] = xp[Ts + r]


def _even_in_sample(x, nw, win, cbuf, h0, cw, cb, wa, ba, wx, bx, lam, *, DB, Ts):
    N = DB * Ts
    return pl.pallas_call(
        functools.partial(_even_in_sample_kernel, DB=DB, Ts=Ts),
        out_shape=[jax.ShapeDtypeStruct((N, DR), F32), jax.ShapeDtypeStruct((N, SBW), F32),
                   jax.ShapeDtypeStruct((N, SBW), F32), jax.ShapeDtypeStruct((N, SBW), F32),
                   jax.ShapeDtypeStruct((DB, DR), F32), jax.ShapeDtypeStruct((3, DB, DR), F32)],
        compiler_params=pltpu.CompilerParams(vmem_limit_bytes=VMEM_LIMIT),
        name="even_in_sample",
    )(x, nw, win, cbuf, h0, cw, cb, wa, ba, wx, bx, lam)


def _attn_prompt_kernel(qi_ref, kb_ref, q_ref, k_ref, v_ref, badd_ref, tri_ref, o_ref, qs, acc, car, *, tq):
    p = pl.program_id(1)
    qi = qi_ref[p]
    kb = kb_ref[p]
    is_diag = kb == qi
    lane = lax.broadcasted_iota(jnp.int32, (tq, LANES), 1)

    @pl.when(is_diag)
    def _():
        acc[...] = jnp.zeros(acc.shape, F32)
        car[...] = jnp.zeros(car.shape, F32)
        for h in range(NH):
            pr, half = h // 2, h % 2
            own = (lane >= HD) if half == 1 else (lane < HD)
            qp = q_ref[:, pr * LANES:(pr + 1) * LANES]
            qs[h] = (jnp.where(own, qp, 0.0) * (LOG2E * HD ** -0.5)).astype(BF16)

    flag = is_diag.astype(jnp.int32)
    tri = tri_ref[...]
    heads = range(NH)

    def scores(h):
        return lax.dot_general(qs[h], k_ref[:, (h // 2) * LANES:(h // 2 + 1) * LANES], (((1,), (1,)), ((), ())),
                               preferred_element_type=F32)

    ahead = 2
    ss = {h: scores(h) for h in range(ahead)}
    lbs, lks, laters = [], [], []
    for h in heads:
        z = ss.pop(h) + badd_ref[flag, h]
        neg_abs = pltpu.bitcast(pltpu.bitcast(z, jnp.int32) | SIGN_BIT, F32)
        lb = jnp.minimum(z, 0.0) - jnp.log2(1.0 + jnp.exp2(neg_abs))
        lk = lb - z
        lbs.append(lb)
        lks.append(lk)
        laters.append(jnp.dot(lk.astype(BF16), tri, preferred_element_type=F32))
        if h + ahead < NH:
            ss[h + ahead] = scores(h + ahead)
    for h in heads:
        ch = car[h]
        a = jnp.exp2((lbs[h] + ch) + laters[h])
        acc[h] += jnp.dot(a.astype(BF16), v_ref[:, (h // 2) * LANES:(h // 2 + 1) * LANES],
                          preferred_element_type=F32)
        car[h] = ch + (laters[h][:, 0:1] + lks[h][:, 0:1])

    @pl.when(kb == 0)
    def _():
        for pr in range(NH // 2):
            o_ref[:, pr * LANES:(pr + 1) * LANES] = jnp.where(lane < HD, acc[2 * pr], acc[2 * pr + 1])


def _attn_prompt(q, k16, v16, bias, *, B, T, tq):
    nq = T // tq
    qi_tab, kb_tab = [], []
    for qi in range(nq):
        for kb in range(qi, -1, -1):
            qi_tab.append(qi)
            kb_tab.append(kb)
    npairs = len(qi_tab)
    qi_tab = jnp.asarray(np.array(qi_tab, np.int32))
    kb_tab = jnp.asarray(np.array(kb_tab, np.int32))
    causal = np.arange(tq)[None, :] < np.arange(tq)[:, None]
    b2 = (bias.astype(F32) * LOG2E)[:, None, None]
    badd = jnp.stack([jnp.broadcast_to(b2, (NH, tq, tq)), jnp.where(jnp.asarray(causal)[None], b2, MASKED_Z)])
    tri = (np.arange(tq)[:, None] > np.arange(tq)[None, :]).astype(np.float32)
    tri = jnp.asarray(tri, BF16)
    const = lambda *idx: (lambda b, p, qt, kt: idx)
    grid_spec = pltpu.PrefetchScalarGridSpec(
        num_scalar_prefetch=2,
        grid=(B, npairs),
        in_specs=[pl.BlockSpec((tq, SBW), lambda b, p, qt, kt: (b * nq + qt[p], 0)),
                  pl.BlockSpec((tq, SBW), lambda b, p, qt, kt: (b * nq + kt[p], 0)),
                  pl.BlockSpec((tq, SBW), lambda b, p, qt, kt: (b * nq + kt[p], 0)),
                  pl.BlockSpec((2, NH, tq, tq), const(0, 0, 0, 0), pipeline_mode=pl.Buffered(1)),
                  pl.BlockSpec((tq, tq), const(0, 0), pipeline_mode=pl.Buffered(1))],
        out_specs=pl.BlockSpec((tq, SBW), lambda b, p, qt, kt: (b * nq + qt[p], 0)),
        scratch_shapes=[pltpu.VMEM((NH, tq, LANES), BF16), pltpu.VMEM((NH, tq, LANES), F32),
                        pltpu.VMEM((NH, tq, 1), F32)],
    )
    return pl.pallas_call(
        functools.partial(_attn_prompt_kernel, tq=tq),
        grid_spec=grid_spec,
        out_shape=jax.ShapeDtypeStruct((B * T, SBW), F32),
        compiler_params=_cp("arbitrary", "arbitrary"),
        name="attn_prompt",
    )(qi_tab, kb_tab, q, k16, v16, badd, tri)


def _attn_sample_kernel(pt_ref, q_ref, kn_ref, vn_ref, zb_ref, zbn_ref, tri_ref, trin_ref, *refs, Ts, pps, page):
    kp_refs = refs[0:pps]
    vp_refs = refs[pps:2 * pps]
    o_ref = refs[2 * pps]
    acc, car = refs[2 * pps + 1], refs[2 * pps + 2]
    s = pl.program_id(1)
    R = Ts * NH
    q = (q_ref[0] * (LOG2E * HD ** -0.5)).astype(BF16)

    def local_terms(krows, zb, tri, blk):
        nblk = krows.shape[0] // blk
        z = _dot_nt(q, krows) + zb
        neg_abs = pltpu.bitcast(pltpu.bitcast(z, jnp.int32) | SIGN_BIT, F32)
        lb = jnp.minimum(z, 0.0) - jnp.log2(1.0 + jnp.exp2(neg_abs))
        lk = lb - z
        pieces = [lk[:, j * blk:(j + 1) * blk] for j in range(nblk)]
        stacked = jnp.concatenate(pieces, axis=0) if nblk > 1 else lk
        lat = jnp.dot(stacked.astype(BF16), tri, preferred_element_type=F32)
        lats = [lat[j * R:(j + 1) * R] for j in range(nblk)]
        tots = [lj[:, 0:1] + pj[:, 0:1] for lj, pj in zip(lats, pieces)]
        return lb, lats, tots

    def weights(lb, lats, tots, run):
        outs = [None] * len(lats)
        for j in reversed(range(len(lats))):
            outs[j] = lats[j] + run
            run = run + tots[j]
        later = jnp.concatenate(outs, axis=1) if len(outs) > 1 else outs[0]
        return jnp.exp2(lb + later), run

    @pl.when(s == 0)
    def _():
        lb, lats, tots = local_terms(kn_ref[0], zbn_ref[...], trin_ref[...], R)
        a, run = weights(lb, lats, tots, jnp.zeros((R, 1), F32))
        acc[...] = _dot(a, vn_ref[0])
        car[...] = run

    terms = [local_terms(kp_refs[r][0, 0].reshape(page * NH, HD), zb_ref[...], tri_ref[...], LANES)
             for r in range(pps)]
    run = car[...]
    total = acc[...]
    for r in range(pps):
        a, run = weights(*terms[r], run)
        total = total + _dot(a, vp_refs[r][0, 0].reshape(page * NH, HD))
    acc[...] = total
    car[...] = run

    @pl.when(s == pl.num_programs(1) - 1)
    def _():
        o_ref[0] = acc[...]


def _attn_sample(q_seq, k_seq, v_seq, bias, pool_k, pool_v, page_table, *, layer, DB, Ts, pps):
    n_pages = page_table.shape[1]
    page = pool_k.shape[2]
    R = Ts * NH
    nsteps = n_pages // pps
    rows = lambda a: a.reshape(DB, R, HD)
    b2 = jnp.tile(bias.astype(F32) * LOG2E, Ts)[:, None]
    rh = np.arange(R) % NH
    rt = np.arange(R) // NH
    zb = jnp.where(jnp.asarray(rh[:, None] == (np.arange(page * NH) % NH)[None, :]), b2, MASKED_Z)
    zbn = jnp.where(jnp.asarray((rh[:, None] == rh[None, :]) & (rt[None, :] < rt[:, None])), b2, MASKED_Z)
    tri = jnp.asarray((np.arange(LANES)[:, None] > np.arange(LANES)[None, :]).astype(np.float32), BF16)
    trin = jnp.asarray((np.arange(R)[:, None] > np.arange(R)[None, :]).astype(np.float32), BF16)

    def pool_map(r):
        return lambda b, s, pt: (layer, pt[b, n_pages - 1 - (s * pps + r)], 0, 0, 0)

    seq = lambda b, s, pt: (b, 0, 0)
    const = lambda b, s, pt: (0, 0)
    pool_spec = lambda r: pl.BlockSpec((1, 1, page, NH, HD), pool_map(r))
    grid_spec = pltpu.PrefetchScalarGridSpec(
        num_scalar_prefetch=1,
        grid=(DB, nsteps),
        in_specs=([pl.BlockSpec((1, R, HD), seq), pl.BlockSpec((1, R, HD), seq), pl.BlockSpec((1, R, HD), seq),
                   pl.BlockSpec((R, page * NH), const), pl.BlockSpec((R, R), const),
                   pl.BlockSpec((LANES, LANES), const), pl.BlockSpec((R, R), const)]
                  + [pool_spec(r) for r in range(pps)] + [pool_spec(r) for r in range(pps)]),
        out_specs=pl.BlockSpec((1, R, HD), seq),
        scratch_shapes=[pltpu.VMEM((R, HD), F32), pltpu.VMEM((R, 1), F32)],
    )
    return pl.pallas_call(
        functools.partial(_attn_sample_kernel, Ts=Ts, pps=pps, page=page),
        grid_spec=grid_spec,
        out_shape=jax.ShapeDtypeStruct((DB, R, HD), F32),
        compiler_params=_cp("arbitrary", "arbitrary"),
        name="attn_sample",
    )(page_table, rows(q_seq), rows(k_seq), rows(v_seq), zb, zbn, tri, trin, *([pool_k] * pps), *([pool_v] * pps))


def _mix_out_kernel(*refs, nparts):
    x_ref = refs[0]
    parts = refs[1:1 + nparts]
    w_ref, nw_ref, o_ref = refs[1 + nparts], refs[2 + nparts], refs[3 + nparts]
    off = 0
    y = None
    for pr in parts:
        kdim = pr.shape[1]
        t = jnp.dot(pr[...].astype(BF16), w_ref[off:off + kdim, :], preferred_element_type=F32)
        y = t if y is None else y + t
        off += kdim
    o_ref[...] = x_ref[...] + _rms(y, nw_ref[...])


def _mix_out(x, parts, w, nw, *, tm):
    N = x.shape[0]
    row = lambda i: (i, 0)
    const = lambda i: (0, 0)
    return pl.pallas_call(
        functools.partial(_mix_out_kernel, nparts=len(parts)),
        grid=(N // tm,),
        in_specs=([pl.BlockSpec((tm, D), row)] + [pl.BlockSpec((tm, p.shape[1]), row) for p in parts]
                  + [pl.BlockSpec(w.shape, const), pl.BlockSpec((1, D), const)]),
        out_specs=pl.BlockSpec((tm, D), row),
        out_shape=jax.ShapeDtypeStruct((N, D), F32),
        compiler_params=_cp("arbitrary"),
        name="mix_out",
    )(x, *parts, w, nw)


def _ffn_prompt_kernel(x_ref, nwa_ref, nwb_ref, wg_ref, wv_ref, cwg_ref, cwv_ref, cbg_ref, cbv_ref, wd_ref,
                       o_ref, tg_ref, tv_ref, xn, acc, halo_g, halo_v, ubuf, *, tm, tiles_per_seq):
    i = pl.program_id(0)
    k = pl.program_id(1)

    @pl.when(k == 0)
    def _():
        xn[...] = _rms(x_ref[...], nwa_ref[...]).astype(BF16)
        acc[...] = jnp.zeros(acc.shape, F32)

    seq_start = (i % tiles_per_seq) == 0
    tf = wg_ref.shape[1]
    nsub = tf // FF_SUB
    xnv = xn[...]

    us = []
    for c in range(nsub):
        cs = slice(c * FF_SUB, (c + 1) * FF_SUB)
        for which, (w_ref, halo, t_ref) in enumerate(((wg_ref, halo_g, tg_ref), (wv_ref, halo_v, tv_ref))):
            u = jnp.dot(xnv, w_ref[:, cs], preferred_element_type=F32)
            slot = 2 * c + which
            ubuf[slot, 0:8, :] = jnp.where(seq_start, 0.0, halo[k, :, cs])
            ubuf[slot, 8:tm + 8, :] = u
            halo[k, :, cs] = u[tm - 8:tm, :]
            t_ref[0, :, cs] = u[tm - 2:tm, :]
            us.append(u)

    def conv(slot, cs, cw_ref, cb_ref):
        cw = cw_ref[:, cs]
        return (cw[2:3] * us[slot] + cw[1:2] * ubuf[slot, pl.ds(7, tm), :] + cw[0:1] * ubuf[slot, pl.ds(6, tm), :]
                + cb_ref[:, cs])

    hs = []
    for c in range(nsub):
        cs = slice(c * FF_SUB, (c + 1) * FF_SUB)
        cg = conv(2 * c, cs, cwg_ref, cbg_ref)
        cv = conv(2 * c + 1, cs, cwv_ref, cbv_ref)
        hs.append((jax.nn.gelu(cg) * cv).astype(BF16))
    hmid = jnp.concatenate(hs, axis=1) if nsub > 1 else hs[0]
    acc[...] += jnp.dot(hmid, wd_ref[...], preferred_element_type=F32)

    @pl.when(k == pl.num_programs(1) - 1)
    def _():
        o_ref[...] = x_ref[...] + _rms(acc[...], nwb_ref[...])


def _ffn_prompt(x, nwa, nwb, wup, cw, cb, wdn, *, T, tm, tf):
    N = x.shape[0]
    nk = DFF // tf
    ntiles = N // tm
    row = lambda i, k: (i, 0)
    const = lambda i, k: (0, 0)
    return pl.pallas_call(
        functools.partial(_ffn_prompt_kernel, tm=tm, tiles_per_seq=T // tm),
        grid=(ntiles, nk),
        in_specs=[pl.BlockSpec((tm, D), row), pl.BlockSpec((1, D), const), pl.BlockSpec((1, D), const),
                  pl.BlockSpec((D, tf), lambda i, k: (0, k)), pl.BlockSpec((D, tf), lambda i, k: (0, nk + k)),
                  pl.BlockSpec((3, tf), lambda i, k: (0, k)), pl.BlockSpec((3, tf), lambda i, k: (0, nk + k)),
                  pl.BlockSpec((1, tf), lambda i, k: (0, k)), pl.BlockSpec((1, tf), lambda i, k: (0, nk + k)),
                  pl.BlockSpec((tf, D), lambda i, k: (k, 0))],
        out_specs=[pl.BlockSpec((tm, D), row), pl.BlockSpec((1, 2, tf), lambda i, k: (i, 0, k)),
                   pl.BlockSpec((1, 2, tf), lambda i, k: (i, 0, k))],
        out_shape=[jax.ShapeDtypeStruct((N, D), F32), jax.ShapeDtypeStruct((ntiles, 2, DFF), F32),
                   jax.ShapeDtypeStruct((ntiles, 2, DFF), F32)],
        scratch_shapes=[pltpu.VMEM((tm, D), BF16), pltpu.VMEM((tm, D), F32), pltpu.VMEM((nk, 8, tf), F32),
                        pltpu.VMEM((nk, 8, tf), F32), pltpu.VMEM((2 * (tf // FF_SUB), tm + 8, FF_SUB), F32)],
        compiler_params=_cp("arbitrary", "arbitrary"),
        name="ffn_prompt",
    )(x, nwa, nwb, wup, wup, cw, cw, cb, cb, wdn)


def _ffn_sample_kernel(x_ref, nwa_ref, nwb_ref, wg_ref, wv_ref, cwg_ref, cwv_ref, cbg_ref, cbv_ref, wd_ref,
                       b0g_ref, b0v_ref, b1g_ref, b1v_ref, o_ref, n0g_ref, n0v_ref, n1g_ref, n1v_ref,
                       xn, acc, *, DB, Ts):
    k = pl.program_id(0)

    @pl.when(k == 0)
    def _():
        xn[...] = _rms(x_ref[...], nwa_ref[...]).astype(BF16)
        acc[...] = jnp.zeros(acc.shape, F32)

    def conv(w_ref, cw_ref, cb_ref, b0_ref, b1_ref, n0_ref, n1_ref):
        u = jnp.dot(xn[...], w_ref[...], preferred_element_type=F32)
        xp = [b0_ref[...], b1_ref[...]] + [u[t * DB:(t + 1) * DB] for t in range(Ts)]
        n0_ref[...] = xp[Ts]
        n1_ref[...] = xp[Ts + 1]
        cw = cw_ref[...]
        return jnp.concatenate(
            [cw[0:1] * xp[t] + cw[1:2] * xp[t + 1] + cw[2:3] * xp[t + 2] + cb_ref[...] for t in range(Ts)], axis=0)

    cg = conv(wg_ref, cwg_ref, cbg_ref, b0g_ref, b1g_ref, n0g_ref, n1g_ref)
    cv = conv(wv_ref, cwv_ref, cbv_ref, b0v_ref, b1v_ref, n0v_ref, n1v_ref)
    hmid = jax.nn.gelu(cg) * cv
    acc[...] += jnp.dot(hmid.astype(BF16), wd_ref[...], preferred_element_type=F32)

    @pl.when(k == pl.num_programs(0) - 1)
    def _():
        o_ref[...] = x_ref[...] + _rms(acc[...], nwb_ref[...])


def _ffn_sample(x, nwa, nwb, wup, cw, cb, wdn, fbuf, *, layer, DB, Ts, tf):
    N = DB * Ts
    nk = DFF // tf
    const = lambda k: (0, 0)
    piece = jax.ShapeDtypeStruct((DB, DFF), F32)
    return pl.pallas_call(
        functools.partial(_ffn_sample_kernel, DB=DB, Ts=Ts),
        grid=(nk,),
        in_specs=[pl.BlockSpec((N, D), const), pl.BlockSpec((1, D), const), pl.BlockSpec((1, D), const),
                  pl.BlockSpec((D, tf), lambda k: (0, k)), pl.BlockSpec((D, tf), lambda k: (0, nk + k)),
                  pl.BlockSpec((3, tf), lambda k: (0, k)), pl.BlockSpec((3, tf), lambda k: (0, nk + k)),
                  pl.BlockSpec((1, tf), lambda k: (0, k)), pl.BlockSpec((1, tf), lambda k: (0, nk + k)),
                  pl.BlockSpec((tf, D), lambda k: (k, 0)),
                  pl.BlockSpec((DB, tf), lambda k: (layer, k)), pl.BlockSpec((DB, tf), lambda k: (layer, nk + k)),
                  pl.BlockSpec((DB, tf), lambda k: (layer, 2 * nk + k)),
                  pl.BlockSpec((DB, tf), lambda k: (layer, 3 * nk + k))],
        out_specs=[pl.BlockSpec((N, D), const)] + [pl.BlockSpec((DB, tf), lambda k: (0, k))] * 4,
        out_shape=[jax.ShapeDtypeStruct((N, D), F32), piece, piece, piece, piece],
        scratch_shapes=[pltpu.VMEM((N, D), BF16), pltpu.VMEM((N, D), F32)],
        compiler_params=_cp("arbitrary"),
        name="ffn_sample",
    )(x, nwa, nwb, wup, wup, cw, cw, cb, cb, wdn, fbuf, fbuf, fbuf, fbuf)


def _odd_post(c, ba, alog, dtb, g_ref, gt_ref):
    c = c * jax.nn.sigmoid(c)
    q = c[:, 0:KW]
    k = c[:, KW:2 * KW]
    v = c[:, 2 * KW:CCH]
    G = g_ref[...]
    GT = gt_ref[...]
    q = q * _dot2(lax.rsqrt(_dot2(q * q, G) + EPS), GT)
    k = k * _dot2(lax.rsqrt(_dot2(k * k, G) + EPS), GT)
    lane = lax.broadcasted_iota(jnp.int32, ba.shape, 1)
    beta = jax.nn.sigmoid(ba)
    g = -jnp.exp(alog) * jax.nn.softplus(ba + dtb)
    gates = jnp.where(lane < NH, beta, g)
    return q, k, v, gates


def _odd_in_prompt_kernel(x_ref, nw_ref, win_ref, cw_ref, alog_ref, dtb_ref, g_ref, gt_ref,
                          q_ref, k_ref, v_ref, z_ref, gates_ref, tail_ref, cbuf, *, tm):
    i = pl.program_id(1)
    xn = _rms(x_ref[...], nw_ref[...])
    proj = jnp.dot(xn.astype(BF16), win_ref[...], preferred_element_type=F32)
    qkv = proj[:, 0:CCH]
    z_ref[...] = proj[:, CCH:CCH + VW]
    ba = proj[:, CCH + VW:O_PAD]

    @pl.when(i == 0)
    def _():
        cbuf[0:8, :] = jnp.zeros((8, CCH), F32)

    @pl.when(i > 0)
    def _():
        cbuf[0:8, :] = cbuf[tm:tm + 8, :]

    cbuf[8:tm + 8, :] = qkv
    cw = cw_ref[...]
    c = (cw[3:4] * qkv + cw[2:3] * cbuf[pl.ds(7, tm), :] + cw[1:2] * cbuf[pl.ds(6, tm), :]
         + cw[0:1] * cbuf[pl.ds(5, tm), :])
    q, k, v, gates = _odd_post(c, ba, alog_ref[...], dtb_ref[...], g_ref, gt_ref)
    q_ref[...] = q
    k_ref[...] = k
    v_ref[...] = v
    gates_ref[...] = gates
    tail_ref[0] = cbuf[pl.ds(tm + 5, 3), :]


def _odd_in_prompt(x, nw, win, cw, alog, dtb, G, GT, *, B, T, tm):
    nt = T // tm
    N = B * T
    row = lambda b, i: (b * nt + i, 0)
    const = lambda b, i: (0, 0)
    return pl.pallas_call(
        functools.partial(_odd_in_prompt_kernel, tm=tm),
        grid=(B, nt),
        in_specs=[pl.BlockSpec((tm, D), row), pl.BlockSpec((1, D), const), pl.BlockSpec((D, O_PAD), const),
                  pl.BlockSpec((4, CCH), const), pl.BlockSpec((1, LANES), const), pl.BlockSpec((1, LANES), const),
                  pl.BlockSpec((KW, LANES), const), pl.BlockSpec((LANES, KW), const)],
        out_specs=[pl.BlockSpec((tm, KW), row), pl.BlockSpec((tm, KW), row), pl.BlockSpec((tm, VW), row),
                   pl.BlockSpec((tm, VW), row), pl.BlockSpec((tm, LANES), row),
                   pl.BlockSpec((1, 3, CCH), lambda b, i: (b, 0, 0))],
        out_shape=[jax.ShapeDtypeStruct((N, KW), F32), jax.ShapeDtypeStruct((N, KW), F32),
                   jax.ShapeDtypeStruct((N, VW), F32), jax.ShapeDtypeStruct((N, VW), F32),
                   jax.ShapeDtypeStruct((N, LANES), F32), jax.ShapeDtypeStruct((B, 3, CCH), F32)],
        scratch_shapes=[pltpu.VMEM((tm + 8, CCH), F32)],
        compiler_params=_cp("arbitrary", "arbitrary"),
        name="odd_in_prompt",
    )(x, nw, win, cw, alog, dtb, G, GT)


def _odd_in_sample_kernel(x_ref, nw_ref, win_ref, cbuf_ref, cw_ref, alog_ref, dtb_ref, g_ref, gt_ref,
                          q_ref, k_ref, v_ref, z_ref, gates_ref, tail_ref, *, DB, Ts):
    xn = _rms(x_ref[...], nw_ref[...])
    proj = jnp.dot(xn.astype(BF16), win_ref[...], preferred_element_type=F32)
    qkv = proj[:, 0:CCH]
    z_ref[...] = proj[:, CCH:CCH + VW]
    ba = proj[:, CCH + VW:O_PAD]
    xp = [cbuf_ref[0], cbuf_ref[1], cbuf_ref[2]] + [qkv[t * DB:(t + 1) * DB] for t in range(Ts)]
    cw = cw_ref[...]
    c = jnp.concatenate(
        [cw[0:1] * xp[t] + cw[1:2] * xp[t + 1] + cw[2:3] * xp[t + 2] + cw[3:4] * xp[t + 3] for t in range(Ts)],
        axis=0)
    q, k, v, gates = _odd_post(c, ba, alog_ref[...], dtb_ref[...], g_ref, gt_ref)
    q_ref[...] = q
    k_ref[...] = k
    v_ref[...] = v
    gates_ref[...] = gates
    for r in range(3):
        tail_ref[r] = xp[Ts + r]


def _odd_in_sample(x, nw, win, cbuf, cw, alog, dtb, G, GT, *, DB, Ts):
    N = DB * Ts
    return pl.pallas_call(
        functools.partial(_odd_in_sample_kernel, DB=DB, Ts=Ts),
        out_shape=[jax.ShapeDtypeStruct((N, KW), F32), jax.ShapeDtypeStruct((N, KW), F32),
                   jax.ShapeDtypeStruct((N, VW), F32), jax.ShapeDtypeStruct((N, VW), F32),
                   jax.ShapeDtypeStruct((N, LANES), F32), jax.ShapeDtypeStruct((3, DB, CCH), F32)],
        compiler_params=pltpu.CompilerParams(vmem_limit_bytes=VMEM_LIMIT),
        name="odd_in_sample",
    )(x, nw, win, cbuf, cw, alog, dtb, G, GT)


def _delta_kernel(q_ref, k_ref, v_ref, z_ref, gates_ref, s0_ref, nw_ref, o_ref, sout_ref, S, *, C, nb):
    c_idx = pl.program_id(1)

    @pl.when(c_idx == 0)
    def _():
        S[...] = s0_ref[...]

    row = lax.broadcasted_iota(jnp.int32, (C, C), 0)
    col = lax.broadcasted_iota(jnp.int32, (C, C), 1)
    incl = row >= col
    strict = row > col
    nw = nw_ref[...]
    chains = [(b, h) for b in range(nb) for h in range(NH)]
    gates, gcs, gcs_t = [], [], []
    for b in range(nb):
        g = gates_ref[b]
        gc_all = _cumsum_rows(g)
        gpad = jnp.concatenate([gc_all, jnp.zeros((LANES - C, LANES), F32)], axis=0) if C < LANES else gc_all
        gates.append(g)
        gcs.append(gc_all)
        gcs_t.append(gpad.T)
    qs, kss, gc_, decay_, pws, sols = [], [], [], [], [], []
    for b, h in chains:
        qh = q_ref[b, :, h * DK:(h + 1) * DK] * (DK ** -0.5)
        kh = k_ref[b, :, h * DK:(h + 1) * DK]
        vh = v_ref[b, :, h * DV:(h + 1) * DV]
        beta = gates[b][:, h:h + 1]
        gc = gcs[b][:, NH + h:NH + h + 1]
        gr = gcs_t[b][NH + h:NH + h + 1, 0:C]
        decay = jnp.where(incl, jnp.exp(jnp.where(incl, gc - gr, 0.0)), 0.0)
        kb = kh * beta
        lm = jnp.where(strict, _dot_nt(kb, kh) * decay, 0.0)
        rhs = jnp.concatenate([vh * beta, kb * jnp.exp(gc)], axis=-1)
        qs.append(qh); kss.append(kh); gc_.append(gc); decay_.append(decay)
        pws.append(-lm)
        sols.append(rhs)
    sols = [s + _dot3(p, s) for p, s in zip(pws, sols)]
    n = 2
    while n < C:
        pws = [_dot3(p, p) for p in pws]
        sols = [s + _dot3(p, s) for p, s in zip(pws, sols)]
        n *= 2
    attns = [_dot_nt(qh, kh) * d for qh, kh, d in zip(qs, kss, decay_)]
    Ss = [S[b, h] for b, h in chains]
    v_news = [sol[:, 0:DV] - _dot(sol[:, DV:DV + DK], Sh) for sol, Sh in zip(sols, Ss)]
    os_ = [_dot(qh * jnp.exp(gc), Sh) + _dot(at, vn)
           for qh, gc, Sh, at, vn in zip(qs, gc_, Ss, attns, v_news)]
    for (b, h), kh, gc, Sh, vn, o in zip(chains, kss, gc_, Ss, v_news, os_):
        g_last = gc[C - 1:C, :]
        S[b, h] = Sh * jnp.exp(g_last) + _dot_tn(kh * jnp.exp(g_last - gc), vn)
        zh = z_ref[b, :, h * DV:(h + 1) * DV]
        o_ref[b, :, h * DV:(h + 1) * DV] = _rms(o, nw) * (zh * jax.nn.sigmoid(zh))

    @pl.when(c_idx == pl.num_programs(1) - 1)
    def _():
        sout_ref[...] = S[...]


def _delta(q, k, v, z, gates, s0, nw, *, nseq, nchunks, C, nb, s_base):
    Tq = nchunks * C
    blk = lambda ch: pl.BlockSpec((nb, C, ch), lambda b, c: (b, c, 0))
    return pl.pallas_call(
        functools.partial(_delta_kernel, C=C, nb=nb),
        grid=(nseq // nb, nchunks),
        in_specs=[blk(KW), blk(KW), blk(VW), blk(VW), blk(LANES),
                  pl.BlockSpec((nb, NH, DK, DV), lambda b, c: (s_base // nb + b, 0, 0, 0)),
                  pl.BlockSpec((1, DV), lambda b, c: (0, 0))],
        out_specs=[blk(VW), pl.BlockSpec((nb, NH, DK, DV), lambda b, c: (b, 0, 0, 0))],
        out_shape=[jax.ShapeDtypeStruct((nseq, Tq, VW), F32), jax.ShapeDtypeStruct((nseq, NH, DK, DV), F32)],
        scratch_shapes=[pltpu.VMEM((nb, NH, DK, DV), F32)],
        compiler_params=_cp("arbitrary", "arbitrary"),
        name="delta",
    )(q, k, v, z, gates, s0, nw)


def _tmajor_to_seq(a, DB, Ts):
    return a.reshape(Ts, DB, a.shape[-1]).transpose(1, 0, 2)


def _pad_seq(a, DB, Ts, Tp):
    s = _tmajor_to_seq(a, DB, Ts)
    return jnp.pad(s, ((0, 0), (0, Tp - Ts), (0, 0)))


def kernel(x_prompt, x_sample, cache_sb_k, cache_sb_v, state_lru_h, state_lru_conv, state_dn_S, state_dn_conv,
           state_ffn_conv, page_table, norm_mix_pre, norm_mix_post, norm_ffn_pre, norm_ffn_post, w_in_e,
           lru_conv_w, lru_conv_b, lru_wa, lru_ba, lru_wx, lru_bx, lru_lambda, sb_bias, w_out_e, w_in_o,
           dn_conv_w, dn_A_log, dn_dt_bias, dn_norm_w, w_out_o, ffn_w_up, ffn_conv_w, ffn_conv_b, ffn_w_down):
    B, T, d_model = x_prompt.shape
    DB, Ts, _ = x_sample.shape
    depth = norm_mix_pre.shape[0]
    n_even, n_pool, page = cache_sb_k.shape[0], cache_sb_k.shape[1], cache_sb_k.shape[2]
    n_pages = page_table.shape[1]
    assert d_model == D and w_in_e.shape[-1] == E_IN and w_in_o.shape[-1] == O_IN
    assert ffn_w_down.shape[1] == DFF and n_pages * page > 0 and Ts >= 3
    tm = min(512, T)
    tq = min(256, T)
    tf = 1024
    C = min(DN_CHUNK, T)
    Cs = 8
    pps = 8 if n_pages % 8 == 0 else (4 if n_pages % 4 == 0 else 1)
    nbp = 2 if B % 2 == 0 else 1
    nbs = 2 if DB % 2 == 0 else 1
    assert T % tm == 0 and T % tq == 0 and T % C == 0 and Ts <= Cs and (DB * Ts) % 8 == 0

    xp = x_prompt.reshape(B * T, D)
    xs = x_sample.transpose(1, 0, 2).reshape(Ts * DB, D)
    dn_S_all = state_dn_S.reshape(-1, NH, DK, DV)
    fbuf_all = state_ffn_conv.reshape(depth * DB, 2 * 2 * DFF)
    s0_zero = jnp.zeros((B, NH, DK, DV), F32)

    G = jnp.asarray((np.arange(KW)[:, None] // DK == np.arange(LANES)[None, :]).astype(np.float32), BF16)
    GT = G.T

    r1 = lambda a: a.reshape(1, -1)
    ks, vs, hs, lcs, Ss, dcs, fcs = ([[], []] for _ in range(7))
    for l in range(depth):
        j = l // 2
        nw_pre = r1(norm_mix_pre[l])
        if l % 2 == 0:
            win = w_in_e[j].astype(BF16)
            wa = jax.scipy.linalg.block_diag(*lru_wa[j]).astype(BF16)
            wx = jax.scipy.linalg.block_diag(*lru_wx[j]).astype(BF16)
            pe = (lru_conv_w[j], r1(lru_conv_b[j]), wa, r1(lru_ba[j]), wx, r1(lru_bx[j]), r1(lru_lambda[j]))
            yr_p, q_p, k_p, v_p, k16_p, v16_p, hl_p, lc_p = _even_in_prompt(xp, nw_pre, win, *pe, B=B, T=T, tm=tm)
            ya_p = _attn_prompt(q_p, k16_p, v16_p, sb_bias[j], B=B, T=T, tq=tq)
            yr_s, q_s, k_s, v_s, hl_s, lc_s = _even_in_sample(
                xs, nw_pre, win, state_lru_conv[j].transpose(1, 0, 2), state_lru_h[j], *pe, DB=DB, Ts=Ts)
            q_s, k_s, v_s = (_tmajor_to_seq(a, DB, Ts) for a in (q_s, k_s, v_s))
            ya_s = _attn_sample(q_s, k_s, v_s, sb_bias[j], cache_sb_k, cache_sb_v, page_table, layer=j,
                                DB=DB, Ts=Ts, pps=pps)
            ya_s = ya_s.reshape(DB, Ts, SBW).transpose(1, 0, 2).reshape(Ts * DB, SBW)
            wout = w_out_e[j].astype(BF16)
            nw_post = r1(norm_mix_post[l])
            xp = _mix_out(xp, [yr_p, ya_p], wout, nw_post, tm=tm)
            xs = _mix_out(xs, [yr_s, ya_s], wout, nw_post, tm=min(512, Ts * DB))
            ks[0].append(k_p.reshape(B, T, NH, HD)); ks[1].append(k_s.reshape(DB, Ts, NH, HD))
            vs[0].append(v_p.reshape(B, T, NH, HD)); vs[1].append(v_s.reshape(DB, Ts, NH, HD))
            hs[0].append(hl_p.reshape(B, DR)); hs[1].append(hl_s)
            lcs[0].append(lc_p); lcs[1].append(lc_s.transpose(1, 0, 2))
        else:
            win = jnp.pad(w_in_o[j], ((0, 0), (0, O_PAD - O_IN))).astype(BF16)
            alog = jnp.zeros((1, LANES), F32).at[0, NH:2 * NH].set(dn_A_log[j])
            dtb = jnp.zeros((1, LANES), F32).at[0, NH:2 * NH].set(dn_dt_bias[j])
            nwd = r1(dn_norm_w[j])
            q_p, k_p, v_p, z_p, g_p, dc_p = _odd_in_prompt(xp, nw_pre, win, dn_conv_w[j], alog, dtb, G, GT,
                                                           B=B, T=T, tm=tm)
            sq = lambda a: a.reshape(B, T, a.shape[-1])
            o_p, S_p = _delta(sq(q_p), sq(k_p), sq(v_p), sq(z_p), sq(g_p), s0_zero, nwd, nseq=B, nchunks=T // C,
                              C=C, nb=nbp, s_base=0)
            o_p = o_p.reshape(B * T, VW)
            q_s, k_s, v_s, z_s, g_s, dc_s = _odd_in_sample(
                xs, nw_pre, win, state_dn_conv[j].transpose(1, 0, 2), dn_conv_w[j], alog, dtb, G, GT, DB=DB, Ts=Ts)
            q_s, k_s, v_s, z_s, g_s = (_pad_seq(a, DB, Ts, Cs) for a in (q_s, k_s, v_s, z_s, g_s))
            o_s, S_s = _delta(q_s, k_s, v_s, z_s, g_s, dn_S_all, nwd, nseq=DB, nchunks=1, C=Cs, nb=nbs,
                              s_base=j * DB)
            o_s = o_s[:, :Ts].transpose(1, 0, 2).reshape(Ts * DB, VW)
            wout = w_out_o[j].astype(BF16)
            nw_post = r1(norm_mix_post[l])
            xp = _mix_out(xp, [o_p], wout, nw_post, tm=tm)
            xs = _mix_out(xs, [o_s], wout, nw_post, tm=min(512, Ts * DB))
            Ss[0].append(S_p); Ss[1].append(S_s)
            dcs[0].append(dc_p); dcs[1].append(dc_s.transpose(1, 0, 2))
        wup = ffn_w_up[l].astype(BF16)
        wdn = ffn_w_down[l].astype(BF16)
        fa = (r1(norm_ffn_pre[l]), r1(norm_ffn_post[l]), wup, ffn_conv_w[l], r1(ffn_conv_b[l]), wdn)
        xp, tg, tv = _ffn_prompt(xp, *fa, T=T, tm=tm, tf=tf)
        last = (np.arange(B) + 1) * (T // tm) - 1
        fcs[0].append(jnp.concatenate([tg[last], tv[last]], axis=-1))
        xs, n0g, n0v, n1g, n1v = _ffn_sample(xs, *fa, fbuf_all, layer=l, DB=DB, Ts=Ts, tf=tf)
        fcs[1].append(jnp.stack([jnp.concatenate([n0g, n0v], axis=-1), jnp.concatenate([n1g, n1v], axis=-1)], axis=1))

    y_prompt = xp.reshape(B, T, D)
    y_sample = xs.reshape(Ts, DB, D).transpose(1, 0, 2)
    st = jnp.stack
    return (y_prompt, y_sample, st(ks[0]), st(vs[0]), st(ks[1]), st(vs[1]), st(hs[0]), st(hs[1]),
            st(lcs[0]), st(lcs[1]), st(Ss[0]), st(Ss[1]), st(dcs[0]), st(dcs[1]), st(fcs[0]), st(fcs[1]))
```

```python
import functools

import numpy as np
import jax
import jax.numpy as jnp
from jax import lax
from jax.experimental import pallas as pl
from jax.experimental.pallas import tpu as pltpu

F32 = jnp.float32
BF16 = jnp.bfloat16
EPS = 1e-6

D = 1024
DR = 512
SBW = 512
NH = 8
HD = 64
LRU_C = 8.0
E_IN = 2 * DR + 3 * SBW
DK = 64
DV = 128
KW = NH * DK
VW = NH * DV
CCH = 2 * KW + VW
O_IN = CCH + VW + 2 * NH
O_PAD = CCH + VW + 128
DFF = 4096
FF_SUB = 512
DN_CHUNK = 64
LANES = 128
LOG2E = 1.4426950408889634
SIGN_BIT = np.int32(-2 ** 31)
MASKED_Z = -1e30
VMEM_LIMIT = 52 * 1024 * 1024


def _cp(*sem):
    return pltpu.CompilerParams(dimension_semantics=sem, vmem_limit_bytes=VMEM_LIMIT)


def _rms(x, w):
    ms = jnp.mean(x * x, axis=-1, keepdims=True)
    return x * lax.rsqrt(ms + EPS) * w


def _dot(a, b):
    return jnp.dot(a.astype(BF16), b.astype(BF16), preferred_element_type=F32)


def _dot_nt(a, b):
    return lax.dot_general(a.astype(BF16), b.astype(BF16), (((1,), (1,)), ((), ())),
                           preferred_element_type=F32)


def _dot_tn(a, b):
    return lax.dot_general(a.astype(BF16), b.astype(BF16), (((0,), (0,)), ((), ())),
                           preferred_element_type=F32)


def _split(a):
    hi = a.astype(BF16)
    lo = (a - hi.astype(F32)).astype(BF16)
    return hi, lo


def _dot2(a, b_exact):
    hi, lo = _split(a)
    b = b_exact.astype(BF16)
    return (jnp.dot(hi, b, preferred_element_type=F32) + jnp.dot(lo, b, preferred_element_type=F32))


def _dot3(a, b):
    ah, al = _split(a)
    bh, bl = _split(b)
    return (jnp.dot(ah, bh, preferred_element_type=F32) + jnp.dot(ah, bl, preferred_element_type=F32)
            + jnp.dot(al, bh, preferred_element_type=F32))


def _expm1(x):
    return jnp.tanh(0.5 * x) * (jnp.exp(x) + 1.0)


def _scan_rows(a, b):
    n = a.shape[0]
    row = lax.broadcasted_iota(jnp.int32, a.shape, 0)
    s = 1
    while s < n:
        m = row >= s
        a_sh = pltpu.roll(a, s, 0)
        b_sh = pltpu.roll(b, s, 0)
        b = jnp.where(m, a * b_sh + b, b)
        a = jnp.where(m, a * a_sh, a)
        s *= 2
    return a, b


def _cumsum_rows(x):
    n = x.shape[0]
    row = lax.broadcasted_iota(jnp.int32, x.shape, 0)
    s = 1
    while s < n:
        x = x + jnp.where(row >= s, pltpu.roll(x, s, 0), 0.0)
        s *= 2
    return x


def _lru_gates(xc, wa, ba, wx, bx, lam):
    r = jax.nn.sigmoid(_dot(xc, wa) + ba)
    gi = jax.nn.sigmoid(_dot(xc, wx) + bx)
    log_a = -LRU_C * r * jax.nn.softplus(-lam)
    a = jnp.exp(log_a)
    mult = jnp.sqrt(-_expm1(2.0 * log_a))
    return a, gi, mult


def _softplus2(z):
    neg_abs = pltpu.bitcast(pltpu.bitcast(z, jnp.int32) | SIGN_BIT, F32)
    return jnp.maximum(z, 0.0) + jnp.log2(1.0 + jnp.exp2(neg_abs))


def _even_in_prompt_kernel(x_ref, nw_ref, win_ref, cw_ref, cb_ref, wa_ref, ba_ref, wx_ref, bx_ref, lam_ref,
                           yr_ref, q_ref, k_ref, v_ref, k16_ref, v16_ref, hl_ref, tail_ref, xbuf, hcar, *, tm):
    i = pl.program_id(1)
    xn = _rms(x_ref[...], nw_ref[...])
    proj = jnp.dot(xn.astype(BF16), win_ref[...], preferred_element_type=F32)
    xr = proj[:, 0:DR]
    gr = proj[:, DR:2 * DR]
    kf = proj[:, 2 * DR + SBW:2 * DR + 2 * SBW]
    vf = proj[:, 2 * DR + 2 * SBW:2 * DR + 3 * SBW]
    q_ref[...] = proj[:, 2 * DR:2 * DR + SBW]
    k_ref[...] = kf
    v_ref[...] = vf
    k16_ref[...] = kf.astype(BF16)
    v16_ref[...] = vf.astype(BF16)

    @pl.when(i == 0)
    def _():
        xbuf[0:8, :] = jnp.zeros((8, DR), F32)
        hcar[...] = jnp.zeros((1, DR), F32)

    @pl.when(i > 0)
    def _():
        xbuf[0:8, :] = xbuf[tm:tm + 8, :]

    xbuf[8:tm + 8, :] = xr
    cw = cw_ref[...]
    xc = (cw[3:4] * xr + cw[2:3] * xbuf[pl.ds(7, tm), :] + cw[1:2] * xbuf[pl.ds(6, tm), :]
          + cw[0:1] * xbuf[pl.ds(5, tm), :] + cb_ref[...])
    a, gi, mult = _lru_gates(xc, wa_ref[...], ba_ref[...], wx_ref[...], bx_ref[...], lam_ref[...])
    row = lax.broadcasted_iota(jnp.int32, (tm, DR), 0)
    mult = jnp.where(jnp.logical_and(row == 0, i == 0), 1.0, mult)
    b = xc * gi * mult
    pa, hb = _scan_rows(a, b)
    h = hb + pa * hcar[...]
    hcar[...] = h[tm - 1:tm, :]
    yr_ref[...] = h * jax.nn.gelu(gr)
    hl_ref[0] = h[tm - 1:tm, :]
    tail_ref[0] = xbuf[pl.ds(tm + 5, 3), :]


def _even_in_prompt(x, nw, win, cw, cb, wa, ba, wx, bx, lam, *, B, T, tm):
    nt = T // tm
    N = B * T
    row = lambda b, i: (b * nt + i, 0)
    const = lambda b, i: (0, 0)
    outs = pl.pallas_call(
        functools.partial(_even_in_prompt_kernel, tm=tm),
        grid=(B, nt),
        in_specs=[pl.BlockSpec((tm, D), row), pl.BlockSpec((1, D), const), pl.BlockSpec((D, E_IN), const),
                  pl.BlockSpec((4, DR), const), pl.BlockSpec((1, DR), const), pl.BlockSpec((DR, DR), const),
                  pl.BlockSpec((1, DR), const), pl.BlockSpec((DR, DR), const), pl.BlockSpec((1, DR), const),
                  pl.BlockSpec((1, DR), const)],
        out_specs=[pl.BlockSpec((tm, DR), row), pl.BlockSpec((tm, SBW), row), pl.BlockSpec((tm, SBW), row),
                   pl.BlockSpec((tm, SBW), row), pl.BlockSpec((tm, SBW), row), pl.BlockSpec((tm, SBW), row),
                   pl.BlockSpec((1, 1, DR), lambda b, i: (b, 0, 0)),
                   pl.BlockSpec((1, 3, DR), lambda b, i: (b, 0, 0))],
        out_shape=[jax.ShapeDtypeStruct((N, DR), F32), jax.ShapeDtypeStruct((N, SBW), F32),
                   jax.ShapeDtypeStruct((N, SBW), F32), jax.ShapeDtypeStruct((N, SBW), F32),
                   jax.ShapeDtypeStruct((N, SBW), BF16), jax.ShapeDtypeStruct((N, SBW), BF16),
                   jax.ShapeDtypeStruct((B, 1, DR), F32), jax.ShapeDtypeStruct((B, 3, DR), F32)],
        scratch_shapes=[pltpu.VMEM((tm + 8, DR), F32), pltpu.VMEM((1, DR), F32)],
        compiler_params=_cp("arbitrary", "arbitrary"),
        name="even_in_prompt",
    )(x, nw, win, cw, cb, wa, ba, wx, bx, lam)
    return outs


def _even_in_sample_kernel(x_ref, nw_ref, win_ref, cbuf_ref, h0_ref, cw_ref, cb_ref, wa_ref, ba_ref, wx_ref,
                           bx_ref, lam_ref, yr_ref, q_ref, k_ref, v_ref, hl_ref, tail_ref, *, DB, Ts):
    xn = _rms(x_ref[...], nw_ref[...])
    proj = jnp.dot(xn.astype(BF16), win_ref[...], preferred_element_type=F32)
    xr = proj[:, 0:DR]
    gr = proj[:, DR:2 * DR]
    q_ref[...] = proj[:, 2 * DR:2 * DR + SBW]
    k_ref[...] = proj[:, 2 * DR + SBW:2 * DR + 2 * SBW]
    v_ref[...] = proj[:, 2 * DR + 2 * SBW:2 * DR + 3 * SBW]
    xp = [cbuf_ref[0], cbuf_ref[1], cbuf_ref[2]] + [xr[t * DB:(t + 1) * DB] for t in range(Ts)]
    cw = cw_ref[...]
    xc = jnp.concatenate(
        [cw[0:1] * xp[t] + cw[1:2] * xp[t + 1] + cw[2:3] * xp[t + 2] + cw[3:4] * xp[t + 3] + cb_ref[...]
         for t in range(Ts)], axis=0)
    a, gi, mult = _lru_gates(xc, wa_ref[...], ba_ref[...], wx_ref[...], bx_ref[...], lam_ref[...])
    b = xc * gi * mult
    h = h0_ref[...]
    hs = []
    for t in range(Ts):
        h = a[t * DB:(t + 1) * DB] * h + b[t * DB:(t + 1) * DB]
        hs.append(h)
    yr_ref[...] = jnp.concatenate(hs, axis=0) * jax.nn.gelu(gr)
    hl_ref[...] = h
    for r in range(3):
        tail_ref[r] = xp[Ts + r]


def _even_in_sample(x, nw, win, cbuf, h0, cw, cb, wa, ba, wx, bx, lam, *, DB, Ts):
    N = DB * Ts
    return pl.pallas_call(
        functools.partial(_even_in_sample_kernel, DB=DB, Ts=Ts),
        out_shape=[jax.ShapeDtypeStruct((N, DR), F32), jax.ShapeDtypeStruct((N, SBW), F32),
                   jax.ShapeDtypeStruct((N, SBW), F32), jax.ShapeDtypeStruct((N, SBW), F32),
                   jax.ShapeDtypeStruct((DB, DR), F32), jax.ShapeDtypeStruct((3, DB, DR), F32)],
        compiler_params=pltpu.CompilerParams(vmem_limit_bytes=VMEM_LIMIT),
        name="even_in_sample",
    )(x, nw, win, cbuf, h0, cw, cb, wa, ba, wx, bx, lam)


def _attn_prompt_kernel(qi_ref, kb_ref, q_ref, k_ref, v_ref, badd_ref, tri_ref, o_ref, qs, acc, car, *, tq):
    p = pl.program_id(1)
    qi = qi_ref[p]
    kb = kb_ref[p]
    is_diag = kb == qi
    lane = lax.broadcasted_iota(jnp.int32, (tq, LANES), 1)

    @pl.when(is_diag)
    def _():
        acc[...] = jnp.zeros(acc.shape, F32)
        car[...] = jnp.zeros(car.shape, F32)
        for h in range(NH):
            pr, half = h // 2, h % 2
            own = (lane >= HD) if half == 1 else (lane < HD)
            qp = q_ref[:, pr * LANES:(pr + 1) * LANES]
            qs[h] = (jnp.where(own, qp, 0.0) * (LOG2E * HD ** -0.5)).astype(BF16)

    flag = is_diag.astype(jnp.int32)
    tri = tri_ref[...]
    heads = range(NH)

    def scores(h):
        return lax.dot_general(qs[h], k_ref[:, (h // 2) * LANES:(h // 2 + 1) * LANES], (((1,), (1,)), ((), ())),
                               preferred_element_type=F32)

    ahead = 2
    ss = {h: scores(h) for h in range(ahead)}
    zs, sums = [], []
    for h in heads:
        z = ss.pop(h) + badd_ref[flag, h]
        zs.append(z)
        sums.append(jnp.dot(_softplus2(z).astype(BF16), tri, preferred_element_type=F32))
        if h + ahead < NH:
            ss[h + ahead] = scores(h + ahead)
    for h in heads:
        ch = car[h]
        a = jnp.exp2((zs[h] + ch) + sums[h])
        acc[h] += jnp.dot(a.astype(BF16), v_ref[:, (h // 2) * LANES:(h // 2 + 1) * LANES],
                          preferred_element_type=F32)
        car[h] = ch + sums[h][:, 0:1]

    @pl.when(kb == 0)
    def _():
        for pr in range(NH // 2):
            o_ref[:, pr * LANES:(pr + 1) * LANES] = jnp.where(lane < HD, acc[2 * pr], acc[2 * pr + 1])


def _attn_prompt(q, k16, v16, bias, *, B, T, tq):
    nq = T // tq
    qi_tab, kb_tab = [], []
    for qi in range(nq):
        for kb in range(qi, -1, -1):
            qi_tab.append(qi)
            kb_tab.append(kb)
    npairs = len(qi_tab)
    qi_tab = jnp.asarray(np.array(qi_tab, np.int32))
    kb_tab = jnp.asarray(np.array(kb_tab, np.int32))
    causal = np.arange(tq)[None, :] < np.arange(tq)[:, None]
    b2 = (bias.astype(F32) * LOG2E)[:, None, None]
    badd = jnp.stack([jnp.broadcast_to(b2, (NH, tq, tq)), jnp.where(jnp.asarray(causal)[None], b2, MASKED_Z)])
    tri = jnp.asarray(-(np.arange(tq)[:, None] >= np.arange(tq)[None, :]).astype(np.float32), BF16)
    const = lambda *idx: (lambda b, p, qt, kt: idx)
    grid_spec = pltpu.PrefetchScalarGridSpec(
        num_scalar_prefetch=2,
        grid=(B, npairs),
        in_specs=[pl.BlockSpec((tq, SBW), lambda b, p, qt, kt: (b * nq + qt[p], 0)),
                  pl.BlockSpec((tq, SBW), lambda b, p, qt, kt: (b * nq + kt[p], 0)),
                  pl.BlockSpec((tq, SBW), lambda b, p, qt, kt: (b * nq + kt[p], 0)),
                  pl.BlockSpec((2, NH, tq, tq), const(0, 0, 0, 0), pipeline_mode=pl.Buffered(1)),
                  pl.BlockSpec((tq, tq), const(0, 0), pipeline_mode=pl.Buffered(1))],
        out_specs=pl.BlockSpec((tq, SBW), lambda b, p, qt, kt: (b * nq + qt[p], 0)),
        scratch_shapes=[pltpu.VMEM((NH, tq, LANES), BF16), pltpu.VMEM((NH, tq, LANES), F32),
                        pltpu.VMEM((NH, tq, 1), F32)],
    )
    return pl.pallas_call(
        functools.partial(_attn_prompt_kernel, tq=tq),
        grid_spec=grid_spec,
        out_shape=jax.ShapeDtypeStruct((B * T, SBW), F32),
        compiler_params=_cp("arbitrary", "arbitrary"),
        name="attn_prompt",
    )(qi_tab, kb_tab, q, k16, v16, badd, tri)


def _attn_sample_kernel(pt_ref, qrep_ref, kn_ref, vn_ref, bias_ref, tri_ref, *refs, Ts, pps, page):
    kp_refs = refs[0:pps]
    vp_refs = refs[pps:2 * pps]
    o_ref = refs[2 * pps]
    acc, car = refs[2 * pps + 1], refs[2 * pps + 2]
    s = pl.program_id(1)
    R = Ts * NH
    rowh = lax.broadcasted_iota(jnp.int32, (R, SBW), 0)
    lanec = lax.broadcasted_iota(jnp.int32, (R, SBW), 1)
    own = (lanec // HD) == (rowh % NH)
    qe = jnp.where(own, qrep_ref[0], 0.0) * (LOG2E * HD ** -0.5)
    bias = bias_ref[...]

    def sb_terms2(z):
        sp = _softplus2(z)
        return z - sp, -sp

    @pl.when(s == 0)
    def _():
        tq = lax.broadcasted_iota(jnp.int32, (R, 1), 0) // NH
        lbs, lks = [], []
        for j in range(Ts):
            zj = jnp.sum(qe * kn_ref[0, j:j + 1, :], axis=-1, keepdims=True) + bias[:, 0:1]
            lb, lkeep = sb_terms2(zj)
            lbs.append(lb)
            lks.append(jnp.where(tq > j, lkeep, 0.0))
        accv = jnp.zeros((R, SBW), F32)
        later = jnp.zeros((R, 1), F32)
        for j in range(Ts - 1, -1, -1):
            aj = jnp.where(tq > j, jnp.exp2(lbs[j] + later), 0.0)
            accv = accv + aj * vn_ref[0, j:j + 1, :]
            later = later + lks[j]
        acc[...] = accv
        car[...] = later

    qe16 = qe.astype(BF16)
    tri = tri_ref[...]
    terms = []
    for r in range(pps):
        kst = kp_refs[r][0, 0].reshape(SBW, page).astype(BF16)
        z = jnp.dot(qe16, kst, preferred_element_type=F32) + bias
        terms.append((z, jnp.dot(_softplus2(z).astype(BF16), tri, preferred_element_type=F32)))
    run = car[...]
    total = acc[...]
    for r in range(pps):
        z, sm = terms[r]
        a = jnp.exp2(z + (sm + run))
        run = run + sm[:, 0:1]
        total = total + _dot_nt(a, vp_refs[r][0, 0].reshape(SBW, page))
    acc[...] = total
    car[...] = run

    @pl.when(s == pl.num_programs(1) - 1)
    def _():
        o_ref[0] = jnp.sum(jnp.where(own, acc[...], 0.0).reshape(Ts, NH, SBW), axis=1)


def _attn_sample(q_seq, k_seq, v_seq, bias, pool_k, pool_v, page_table, *, layer, DB, Ts, pps):
    n_pages = page_table.shape[1]
    page = pool_k.shape[-1]
    R = Ts * NH
    nsteps = n_pages // pps
    qrep = jnp.broadcast_to(q_seq[:, :, None, :], (DB, Ts, NH, SBW)).reshape(DB, R, SBW)
    bias_rows = jnp.broadcast_to(jnp.tile(bias.astype(F32) * LOG2E, Ts)[:, None], (R, page))
    tri = jnp.asarray(-(np.arange(page)[:, None] >= np.arange(page)[None, :]).astype(np.float32), BF16)

    def pool_map(r):
        return lambda b, s, pt: (layer, pt[b, n_pages - 1 - (s * pps + r)], 0, 0, 0)

    seq = lambda b, s, pt: (b, 0, 0)
    const = lambda b, s, pt: (0, 0)
    pool_spec = lambda r: pl.BlockSpec((1, 1, NH, HD, page), pool_map(r))
    grid_spec = pltpu.PrefetchScalarGridSpec(
        num_scalar_prefetch=1,
        grid=(DB, nsteps),
        in_specs=([pl.BlockSpec((1, R, SBW), seq), pl.BlockSpec((1, Ts, SBW), seq), pl.BlockSpec((1, Ts, SBW), seq),
                   pl.BlockSpec((R, page), const), pl.BlockSpec((page, page), const)]
                  + [pool_spec(r) for r in range(pps)] + [pool_spec(r) for r in range(pps)]),
        out_specs=pl.BlockSpec((1, Ts, SBW), seq),
        scratch_shapes=[pltpu.VMEM((R, SBW), F32), pltpu.VMEM((R, 1), F32)],
    )
    return pl.pallas_call(
        functools.partial(_attn_sample_kernel, Ts=Ts, pps=pps, page=page),
        grid_spec=grid_spec,
        out_shape=jax.ShapeDtypeStruct((DB, Ts, SBW), F32),
        compiler_params=_cp("arbitrary", "arbitrary"),
        name="attn_sample",
    )(page_table, qrep, k_seq, v_seq, bias_rows, tri, *([pool_k] * pps), *([pool_v] * pps))


def _mix_residual(x_ref, part_refs, wo_ref, nwm_ref):
    off = 0
    y = None
    for pr in part_refs:
        kdim = pr.shape[1]
        t = jnp.dot(pr[...].astype(BF16), wo_ref[off:off + kdim, :], preferred_element_type=F32)
        y = t if y is None else y + t
        off += kdim
    return x_ref[...] + _rms(y, nwm_ref[...])


def _ffn_prompt_kernel(*refs, nparts, tm, tiles_per_seq):
    x_ref = refs[0]
    part_refs = refs[1:1 + nparts]
    (wo_ref, nwm_ref, nwa_ref, nwb_ref, wg_ref, wv_ref, cwg_ref, cwv_ref, cbg_ref, cbv_ref, wd_ref,
     o_ref, tg_ref, tv_ref, xmid, xn, acc, halo_g, halo_v, ubuf) = refs[1 + nparts:]
    i = pl.program_id(0)
    k = pl.program_id(1)

    @pl.when(k == 0)
    def _():
        xm = _mix_residual(x_ref, part_refs, wo_ref, nwm_ref)
        xmid[...] = xm
        xn[...] = _rms(xm, nwa_ref[...]).astype(BF16)
        acc[...] = jnp.zeros(acc.shape, F32)

    seq_start = (i % tiles_per_seq) == 0
    tf = wg_ref.shape[1]
    nsub = tf // FF_SUB
    xnv = xn[...]

    us = []
    for c in range(nsub):
        cs = slice(c * FF_SUB, (c + 1) * FF_SUB)
        for which, (w_ref, halo, t_ref) in enumerate(((wg_ref, halo_g, tg_ref), (wv_ref, halo_v, tv_ref))):
            u = jnp.dot(xnv, w_ref[:, cs], preferred_element_type=F32)
            slot = 2 * c + which
            ubuf[slot, 0:8, :] = jnp.where(seq_start, 0.0, halo[k, :, cs])
            ubuf[slot, 8:tm + 8, :] = u
            halo[k, :, cs] = u[tm - 8:tm, :]
            t_ref[0, :, cs] = u[tm - 2:tm, :]
            us.append(u)

    def conv(slot, cs, cw_ref, cb_ref):
        cw = cw_ref[:, cs]
        return (cw[2:3] * us[slot] + cw[1:2] * ubuf[slot, pl.ds(7, tm), :] + cw[0:1] * ubuf[slot, pl.ds(6, tm), :]
                + cb_ref[:, cs])

    hs = []
    for c in range(nsub):
        cs = slice(c * FF_SUB, (c + 1) * FF_SUB)
        cg = conv(2 * c, cs, cwg_ref, cbg_ref)
        cv = conv(2 * c + 1, cs, cwv_ref, cbv_ref)
        hs.append((jax.nn.gelu(cg) * cv).astype(BF16))
    hmid = jnp.concatenate(hs, axis=1) if nsub > 1 else hs[0]
    acc[...] += jnp.dot(hmid, wd_ref[...], preferred_element_type=F32)

    @pl.when(k == pl.num_programs(1) - 1)
    def _():
        o_ref[...] = xmid[...] + _rms(acc[...], nwb_ref[...])


def _ffn_prompt(x, parts, wout, nwm, nwa, nwb, wup, cw, cb, wdn, *, T, tm, tf):
    N = x.shape[0]
    nk = DFF // tf
    ntiles = N // tm
    row = lambda i, k: (i, 0)
    const = lambda i, k: (0, 0)
    return pl.pallas_call(
        functools.partial(_ffn_prompt_kernel, nparts=len(parts), tm=tm, tiles_per_seq=T // tm),
        grid=(ntiles, nk),
        in_specs=[pl.BlockSpec((tm, D), row)] + [pl.BlockSpec((tm, p.shape[1]), row) for p in parts] + [
                  pl.BlockSpec(wout.shape, const), pl.BlockSpec((1, D), const),
                  pl.BlockSpec((1, D), const), pl.BlockSpec((1, D), const),
                  pl.BlockSpec((D, tf), lambda i, k: (0, k)), pl.BlockSpec((D, tf), lambda i, k: (0, nk + k)),
                  pl.BlockSpec((3, tf), lambda i, k: (0, k)), pl.BlockSpec((3, tf), lambda i, k: (0, nk + k)),
                  pl.BlockSpec((1, tf), lambda i, k: (0, k)), pl.BlockSpec((1, tf), lambda i, k: (0, nk + k)),
                  pl.BlockSpec((tf, D), lambda i, k: (k, 0))],
        out_specs=[pl.BlockSpec((tm, D), row), pl.BlockSpec((1, 2, tf), lambda i, k: (i, 0, k)),
                   pl.BlockSpec((1, 2, tf), lambda i, k: (i, 0, k))],
        out_shape=[jax.ShapeDtypeStruct((N, D), F32), jax.ShapeDtypeStruct((ntiles, 2, DFF), F32),
                   jax.ShapeDtypeStruct((ntiles, 2, DFF), F32)],
        scratch_shapes=[pltpu.VMEM((tm, D), F32), pltpu.VMEM((tm, D), BF16), pltpu.VMEM((tm, D), F32),
                        pltpu.VMEM((nk, 8, tf), F32), pltpu.VMEM((nk, 8, tf), F32),
                        pltpu.VMEM((2 * (tf // FF_SUB), tm + 8, FF_SUB), F32)],
        compiler_params=_cp("arbitrary", "arbitrary"),
        name="ffn_prompt",
    )(x, *parts, wout, nwm, nwa, nwb, wup, wup, cw, cw, cb, cb, wdn)


def _ffn_sample_kernel(*refs, nparts, DB, Ts):
    x_ref = refs[0]
    part_refs = refs[1:1 + nparts]
    (wo_ref, nwm_ref, nwa_ref, nwb_ref, wg_ref, wv_ref, cwg_ref, cwv_ref, cbg_ref, cbv_ref, wd_ref,
     b0g_ref, b0v_ref, b1g_ref, b1v_ref, o_ref, n0g_ref, n0v_ref, n1g_ref, n1v_ref,
     xmid, xn, acc) = refs[1 + nparts:]
    k = pl.program_id(0)

    @pl.when(k == 0)
    def _():
        xm = _mix_residual(x_ref, part_refs, wo_ref, nwm_ref)
        xmid[...] = xm
        xn[...] = _rms(xm, nwa_ref[...]).astype(BF16)
        acc[...] = jnp.zeros(acc.shape, F32)

    def conv(w_ref, cw_ref, cb_ref, b0_ref, b1_ref, n0_ref, n1_ref):
        u = jnp.dot(xn[...], w_ref[...], preferred_element_type=F32)
        xp = [b0_ref[...], b1_ref[...]] + [u[t * DB:(t + 1) * DB] for t in range(Ts)]
        n0_ref[...] = xp[Ts]
        n1_ref[...] = xp[Ts + 1]
        cw = cw_ref[...]
        return jnp.concatenate(
            [cw[0:1] * xp[t] + cw[1:2] * xp[t + 1] + cw[2:3] * xp[t + 2] + cb_ref[...] for t in range(Ts)], axis=0)

    cg = conv(wg_ref, cwg_ref, cbg_ref, b0g_ref, b1g_ref, n0g_ref, n1g_ref)
    cv = conv(wv_ref, cwv_ref, cbv_ref, b0v_ref, b1v_ref, n0v_ref, n1v_ref)
    hmid = jax.nn.gelu(cg) * cv
    acc[...] += jnp.dot(hmid.astype(BF16), wd_ref[...], preferred_element_type=F32)

    @pl.when(k == pl.num_programs(0) - 1)
    def _():
        o_ref[...] = xmid[...] + _rms(acc[...], nwb_ref[...])


def _ffn_sample(x, parts, wout, nwm, nwa, nwb, wup, cw, cb, wdn, fbuf, *, layer, DB, Ts, tf):
    N = DB * Ts
    nk = DFF // tf
    const = lambda k: (0, 0)
    piece = jax.ShapeDtypeStruct((DB, DFF), F32)
    return pl.pallas_call(
        functools.partial(_ffn_sample_kernel, nparts=len(parts), DB=DB, Ts=Ts),
        grid=(nk,),
        in_specs=[pl.BlockSpec((N, D), const)] + [pl.BlockSpec((N, p.shape[1]), const) for p in parts] + [
                  pl.BlockSpec(wout.shape, const), pl.BlockSpec((1, D), const),
                  pl.BlockSpec((1, D), const), pl.BlockSpec((1, D), const),
                  pl.BlockSpec((D, tf), lambda k: (0, k)), pl.BlockSpec((D, tf), lambda k: (0, nk + k)),
                  pl.BlockSpec((3, tf), lambda k: (0, k)), pl.BlockSpec((3, tf), lambda k: (0, nk + k)),
                  pl.BlockSpec((1, tf), lambda k: (0, k)), pl.BlockSpec((1, tf), lambda k: (0, nk + k)),
                  pl.BlockSpec((tf, D), lambda k: (k, 0)),
                  pl.BlockSpec((DB, tf), lambda k: (layer, k)), pl.BlockSpec((DB, tf), lambda k: (layer, nk + k)),
                  pl.BlockSpec((DB, tf), lambda k: (layer, 2 * nk + k)),
                  pl.BlockSpec((DB, tf), lambda k: (layer, 3 * nk + k))],
        out_specs=[pl.BlockSpec((N, D), const)] + [pl.BlockSpec((DB, tf), lambda k: (0, k))] * 4,
        out_shape=[jax.ShapeDtypeStruct((N, D), F32), piece, piece, piece, piece],
        scratch_shapes=[pltpu.VMEM((N, D), F32), pltpu.VMEM((N, D), BF16), pltpu.VMEM((N, D), F32)],
        compiler_params=_cp("arbitrary"),
        name="ffn_sample",
    )(x, *parts, wout, nwm, nwa, nwb, wup, wup, cw, cw, cb, cb, wdn, fbuf, fbuf, fbuf, fbuf)


def _odd_post(c, ba, alog, dtb, g_ref, gt_ref):
    c = c * jax.nn.sigmoid(c)
    q = c[:, 0:KW]
    k = c[:, KW:2 * KW]
    v = c[:, 2 * KW:CCH]
    G = g_ref[...]
    GT = gt_ref[...]
    q = q * _dot2(lax.rsqrt(_dot2(q * q, G) + EPS), GT)
    k = k * _dot2(lax.rsqrt(_dot2(k * k, G) + EPS), GT)
    lane = lax.broadcasted_iota(jnp.int32, ba.shape, 1)
    beta = jax.nn.sigmoid(ba)
    g = -jnp.exp(alog) * jax.nn.softplus(ba + dtb)
    gates = jnp.where(lane < NH, beta, g)
    return q, k, v, gates


def _odd_in_prompt_kernel(x_ref, nw_ref, win_ref, cw_ref, alog_ref, dtb_ref, g_ref, gt_ref,
                          q_ref, k_ref, v_ref, z_ref, gates_ref, tail_ref, cbuf, *, tm):
    i = pl.program_id(1)
    xn = _rms(x_ref[...], nw_ref[...])
    proj = jnp.dot(xn.astype(BF16), win_ref[...], preferred_element_type=F32)
    qkv = proj[:, 0:CCH]
    z_ref[...] = proj[:, CCH:CCH + VW]
    ba = proj[:, CCH + VW:O_PAD]

    @pl.when(i == 0)
    def _():
        cbuf[0:8, :] = jnp.zeros((8, CCH), F32)

    @pl.when(i > 0)
    def _():
        cbuf[0:8, :] = cbuf[tm:tm + 8, :]

    cbuf[8:tm + 8, :] = qkv
    cw = cw_ref[...]
    c = (cw[3:4] * qkv + cw[2:3] * cbuf[pl.ds(7, tm), :] + cw[1:2] * cbuf[pl.ds(6, tm), :]
         + cw[0:1] * cbuf[pl.ds(5, tm), :])
    q, k, v, gates = _odd_post(c, ba, alog_ref[...], dtb_ref[...], g_ref, gt_ref)
    q_ref[...] = q
    k_ref[...] = k
    v_ref[...] = v
    gates_ref[...] = gates
    tail_ref[0] = cbuf[pl.ds(tm + 5, 3), :]


def _odd_in_prompt(x, nw, win, cw, alog, dtb, G, GT, *, B, T, tm):
    nt = T // tm
    N = B * T
    row = lambda b, i: (b * nt + i, 0)
    const = lambda b, i: (0, 0)
    return pl.pallas_call(
        functools.partial(_odd_in_prompt_kernel, tm=tm),
        grid=(B, nt),
        in_specs=[pl.BlockSpec((tm, D), row), pl.BlockSpec((1, D), const), pl.BlockSpec((D, O_PAD), const),
                  pl.BlockSpec((4, CCH), const), pl.BlockSpec((1, LANES), const), pl.BlockSpec((1, LANES), const),
                  pl.BlockSpec((KW, LANES), const), pl.BlockSpec((LANES, KW), const)],
        out_specs=[pl.BlockSpec((tm, KW), row), pl.BlockSpec((tm, KW), row), pl.BlockSpec((tm, VW), row),
                   pl.BlockSpec((tm, VW), row), pl.BlockSpec((tm, LANES), row),
                   pl.BlockSpec((1, 3, CCH), lambda b, i: (b, 0, 0))],
        out_shape=[jax.ShapeDtypeStruct((N, KW), F32), jax.ShapeDtypeStruct((N, KW), F32),
                   jax.ShapeDtypeStruct((N, VW), F32), jax.ShapeDtypeStruct((N, VW), F32),
                   jax.ShapeDtypeStruct((N, LANES), F32), jax.ShapeDtypeStruct((B, 3, CCH), F32)],
        scratch_shapes=[pltpu.VMEM((tm + 8, CCH), F32)],
        compiler_params=_cp("arbitrary", "arbitrary"),
        name="odd_in_prompt",
    )(x, nw, win, cw, alog, dtb, G, GT)


def _odd_in_sample_kernel(x_ref, nw_ref, win_ref, cbuf_ref, cw_ref, alog_ref, dtb_ref, g_ref, gt_ref,
                          q_ref, k_ref, v_ref, z_ref, gates_ref, tail_ref, *, DB, Ts):
    xn = _rms(x_ref[...], nw_ref[...])
    proj = jnp.dot(xn.astype(BF16), win_ref[...], preferred_element_type=F32)
    qkv = proj[:, 0:CCH]
    z_ref[...] = proj[:, CCH:CCH + VW]
    ba = proj[:, CCH + VW:O_PAD]
    xp = [cbuf_ref[0], cbuf_ref[1], cbuf_ref[2]] + [qkv[t * DB:(t + 1) * DB] for t in range(Ts)]
    cw = cw_ref[...]
    c = jnp.concatenate(
        [cw[0:1] * xp[t] + cw[1:2] * xp[t + 1] + cw[2:3] * xp[t + 2] + cw[3:4] * xp[t + 3] for t in range(Ts)],
        axis=0)
    q, k, v, gates = _odd_post(c, ba, alog_ref[...], dtb_ref[...], g_ref, gt_ref)
    q_ref[...] = q
    k_ref[...] = k
    v_ref[...] = v
    gates_ref[...] = gates
    for r in range(3):
        tail_ref[r] = xp[Ts + r]


def _odd_in_sample(x, nw, win, cbuf, cw, alog, dtb, G, GT, *, DB, Ts):
    N = DB * Ts
    return pl.pallas_call(
        functools.partial(_odd_in_sample_kernel, DB=DB, Ts=Ts),
        out_shape=[jax.ShapeDtypeStruct((N, KW), F32), jax.ShapeDtypeStruct((N, KW), F32),
                   jax.ShapeDtypeStruct((N, VW), F32), jax.ShapeDtypeStruct((N, VW), F32),
                   jax.ShapeDtypeStruct((N, LANES), F32), jax.ShapeDtypeStruct((3, DB, CCH), F32)],
        compiler_params=pltpu.CompilerParams(vmem_limit_bytes=VMEM_LIMIT),
        name="odd_in_sample",
    )(x, nw, win, cbuf, cw, alog, dtb, G, GT)


def _delta_kernel(q_ref, k_ref, v_ref, z_ref, gates_ref, s0_ref, nw_ref, o_ref, sout_ref, S, *, C, nb):
    c_idx = pl.program_id(1)

    @pl.when(c_idx == 0)
    def _():
        S[...] = s0_ref[...]

    row = lax.broadcasted_iota(jnp.int32, (C, C), 0)
    col = lax.broadcasted_iota(jnp.int32, (C, C), 1)
    incl = row >= col
    strict = row > col
    nw = nw_ref[...]
    chains = [(b, h) for b in range(nb) for h in range(NH)]
    gates, gcs, gcs_t = [], [], []
    for b in range(nb):
        g = gates_ref[b]
        gc_all = _cumsum_rows(g)
        gpad = jnp.concatenate([gc_all, jnp.zeros((LANES - C, LANES), F32)], axis=0) if C < LANES else gc_all
        gates.append(g)
        gcs.append(gc_all)
        gcs_t.append(gpad.T)
    qs, kss, gc_, decay_, pws, sols = [], [], [], [], [], []
    for b, h in chains:
        qh = q_ref[b, :, h * DK:(h + 1) * DK] * (DK ** -0.5)
        kh = k_ref[b, :, h * DK:(h + 1) * DK]
        vh = v_ref[b, :, h * DV:(h + 1) * DV]
        beta = gates[b][:, h:h + 1]
        gc = gcs[b][:, NH + h:NH + h + 1]
        gr = gcs_t[b][NH + h:NH + h + 1, 0:C]
        decay = jnp.where(incl, jnp.exp(jnp.where(incl, gc - gr, 0.0)), 0.0)
        kb = kh * beta
        lm = jnp.where(strict, _dot_nt(kb, kh) * decay, 0.0)
        rhs = jnp.concatenate([vh * beta, kb * jnp.exp(gc)], axis=-1)
        qs.append(qh); kss.append(kh); gc_.append(gc); decay_.append(decay)
        pws.append(-lm)
        sols.append(rhs)
    sols = [s + _dot3(p, s) for p, s in zip(pws, sols)]
    n = 2
    while n < C:
        pws = [_dot3(p, p) for p in pws]
        sols = [s + _dot3(p, s) for p, s in zip(pws, sols)]
        n *= 2
    attns = [_dot_nt(qh, kh) * d for qh, kh, d in zip(qs, kss, decay_)]
    Ss = [S[b, h] for b, h in chains]
    v_news = [sol[:, 0:DV] - _dot(sol[:, DV:DV + DK], Sh) for sol, Sh in zip(sols, Ss)]
    os_ = [_dot(qh * jnp.exp(gc), Sh) + _dot(at, vn)
           for qh, gc, Sh, at, vn in zip(qs, gc_, Ss, attns, v_news)]
    for (b, h), kh, gc, Sh, vn, o in zip(chains, kss, gc_, Ss, v_news, os_):
        g_last = gc[C - 1:C, :]
        S[b, h] = Sh * jnp.exp(g_last) + _dot_tn(kh * jnp.exp(g_last - gc), vn)
        zh = z_ref[b, :, h * DV:(h + 1) * DV]
        o_ref[b, :, h * DV:(h + 1) * DV] = _rms(o, nw) * (zh * jax.nn.sigmoid(zh))

    @pl.when(c_idx == pl.num_programs(1) - 1)
    def _():
        sout_ref[...] = S[...]


def _delta(q, k, v, z, gates, s0, nw, *, nseq, nchunks, C, nb, s_base):
    Tq = nchunks * C
    blk = lambda ch: pl.BlockSpec((nb, C, ch), lambda b, c: (b, c, 0))
    return pl.pallas_call(
        functools.partial(_delta_kernel, C=C, nb=nb),
        grid=(nseq // nb, nchunks),
        in_specs=[blk(KW), blk(KW), blk(VW), blk(VW), blk(LANES),
                  pl.BlockSpec((nb, NH, DK, DV), lambda b, c: (s_base // nb + b, 0, 0, 0)),
                  pl.BlockSpec((1, DV), lambda b, c: (0, 0))],
        out_specs=[blk(VW), pl.BlockSpec((nb, NH, DK, DV), lambda b, c: (b, 0, 0, 0))],
        out_shape=[jax.ShapeDtypeStruct((nseq, Tq, VW), F32), jax.ShapeDtypeStruct((nseq, NH, DK, DV), F32)],
        scratch_shapes=[pltpu.VMEM((nb, NH, DK, DV), F32)],
        compiler_params=_cp("arbitrary", "arbitrary"),
        name="delta",
    )(q, k, v, z, gates, s0, nw)


def _tmajor_to_seq(a, DB, Ts):
    return a.reshape(Ts, DB, a.shape[-1]).transpose(1, 0, 2)


def _pad_seq(a, DB, Ts, Tp):
    s = _tmajor_to_seq(a, DB, Ts)
    return jnp.pad(s, ((0, 0), (0, Tp - Ts), (0, 0)))


def kernel(x_prompt, x_sample, cache_sb_k, cache_sb_v, state_lru_h, state_lru_conv, state_dn_S, state_dn_conv,
           state_ffn_conv, page_table, norm_mix_pre, norm_mix_post, norm_ffn_pre, norm_ffn_post, w_in_e,
           lru_conv_w, lru_conv_b, lru_wa, lru_ba, lru_wx, lru_bx, lru_lambda, sb_bias, w_out_e, w_in_o,
           dn_conv_w, dn_A_log, dn_dt_bias, dn_norm_w, w_out_o, ffn_w_up, ffn_conv_w, ffn_conv_b, ffn_w_down):
    B, T, d_model = x_prompt.shape
    DB, Ts, _ = x_sample.shape
    depth = norm_mix_pre.shape[0]
    n_even, n_pool, page = cache_sb_k.shape[0], cache_sb_k.shape[1], cache_sb_k.shape[2]
    n_pages = page_table.shape[1]
    assert d_model == D and w_in_e.shape[-1] == E_IN and w_in_o.shape[-1] == O_IN
    assert ffn_w_down.shape[1] == DFF and n_pages * page > 0 and Ts >= 3
    tm = min(512, T)
    tq = min(256, T)
    tf = 1024
    C = min(DN_CHUNK, T)
    Cs = 8
    pps = 8 if n_pages % 8 == 0 else (4 if n_pages % 4 == 0 else 1)
    nbp = 2 if B % 2 == 0 else 1
    nbs = 2 if DB % 2 == 0 else 1
    assert T % tm == 0 and T % tq == 0 and T % C == 0 and Ts <= Cs and (DB * Ts) % 8 == 0

    xp = x_prompt.reshape(B * T, D)
    xs = x_sample.transpose(1, 0, 2).reshape(Ts * DB, D)
    pool_k = cache_sb_k.transpose(0, 1, 3, 4, 2)
    pool_v = cache_sb_v.transpose(0, 1, 3, 4, 2)
    dn_S_all = state_dn_S.reshape(-1, NH, DK, DV)
    fbuf_all = state_ffn_conv.reshape(depth * DB, 2 * 2 * DFF)
    s0_zero = jnp.zeros((B, NH, DK, DV), F32)

    G = jnp.asarray((np.arange(KW)[:, None] // DK == np.arange(LANES)[None, :]).astype(np.float32), BF16)
    GT = G.T

    r1 = lambda a: a.reshape(1, -1)
    ks, vs, hs, lcs, Ss, dcs, fcs = ([[], []] for _ in range(7))
    for l in range(depth):
        j = l // 2
        nw_pre = r1(norm_mix_pre[l])
        if l % 2 == 0:
            win = w_in_e[j].astype(BF16)
            wa = jax.scipy.linalg.block_diag(*lru_wa[j]).astype(BF16)
            wx = jax.scipy.linalg.block_diag(*lru_wx[j]).astype(BF16)
            pe = (lru_conv_w[j], r1(lru_conv_b[j]), wa, r1(lru_ba[j]), wx, r1(lru_bx[j]), r1(lru_lambda[j]))
            yr_p, q_p, k_p, v_p, k16_p, v16_p, hl_p, lc_p = _even_in_prompt(xp, nw_pre, win, *pe, B=B, T=T, tm=tm)
            ya_p = _attn_prompt(q_p, k16_p, v16_p, sb_bias[j], B=B, T=T, tq=tq)
            yr_s, q_s, k_s, v_s, hl_s, lc_s = _even_in_sample(
                xs, nw_pre, win, state_lru_conv[j].transpose(1, 0, 2), state_lru_h[j], *pe, DB=DB, Ts=Ts)
            q_s, k_s, v_s = (_tmajor_to_seq(a, DB, Ts) for a in (q_s, k_s, v_s))
            ya_s = _attn_sample(q_s, k_s, v_s, sb_bias[j], pool_k, pool_v, page_table, layer=j,
                                DB=DB, Ts=Ts, pps=pps)
            ya_s = ya_s.transpose(1, 0, 2).reshape(Ts * DB, SBW)
            wout = w_out_e[j].astype(BF16)
            parts_p, parts_s = [yr_p, ya_p], [yr_s, ya_s]
            ks[0].append(k_p.reshape(B, T, NH, HD)); ks[1].append(k_s.reshape(DB, Ts, NH, HD))
            vs[0].append(v_p.reshape(B, T, NH, HD)); vs[1].append(v_s.reshape(DB, Ts, NH, HD))
            hs[0].append(hl_p.reshape(B, DR)); hs[1].append(hl_s)
            lcs[0].append(lc_p); lcs[1].append(lc_s.transpose(1, 0, 2))
        else:
            win = jnp.pad(w_in_o[j], ((0, 0), (0, O_PAD - O_IN))).astype(BF16)
            alog = jnp.zeros((1, LANES), F32).at[0, NH:2 * NH].set(dn_A_log[j])
            dtb = jnp.zeros((1, LANES), F32).at[0, NH:2 * NH].set(dn_dt_bias[j])
            nwd = r1(dn_norm_w[j])
            q_p, k_p, v_p, z_p, g_p, dc_p = _odd_in_prompt(xp, nw_pre, win, dn_conv_w[j], alog, dtb, G, GT,
                                                           B=B, T=T, tm=tm)
            sq = lambda a: a.reshape(B, T, a.shape[-1])
            o_p, S_p = _delta(sq(q_p), sq(k_p), sq(v_p), sq(z_p), sq(g_p), s0_zero, nwd, nseq=B, nchunks=T // C,
                              C=C, nb=nbp, s_base=0)
            o_p = o_p.reshape(B * T, VW)
            q_s, k_s, v_s, z_s, g_s, dc_s = _odd_in_sample(
                xs, nw_pre, win, state_dn_conv[j].transpose(1, 0, 2), dn_conv_w[j], alog, dtb, G, GT, DB=DB, Ts=Ts)
            q_s, k_s, v_s, z_s, g_s = (_pad_seq(a, DB, Ts, Cs) for a in (q_s, k_s, v_s, z_s, g_s))
            o_s, S_s = _delta(q_s, k_s, v_s, z_s, g_s, dn_S_all, nwd, nseq=DB, nchunks=1, C=Cs, nb=nbs,
                              s_base=j * DB)
            o_s = o_s[:, :Ts].transpose(1, 0, 2).reshape(Ts * DB, VW)
            wout = w_out_o[j].astype(BF16)
            parts_p, parts_s = [o_p], [o_s]
            Ss[0].append(S_p); Ss[1].append(S_s)
            dcs[0].append(dc_p); dcs[1].append(dc_s.transpose(1, 0, 2))
        wup = ffn_w_up[l].astype(BF16)
        wdn = ffn_w_down[l].astype(BF16)
        fa = (wout, r1(norm_mix_post[l]), r1(norm_ffn_pre[l]), r1(norm_ffn_post[l]), wup, ffn_conv_w[l],
              r1(ffn_conv_b[l]), wdn)
        xp, tg, tv = _ffn_prompt(xp, parts_p, *fa, T=T, tm=tm, tf=tf)
        last = (np.arange(B) + 1) * (T // tm) - 1
        fcs[0].append(jnp.concatenate([tg[last], tv[last]], axis=-1))
        xs, n0g, n0v, n1g, n1v = _ffn_sample(xs, parts_s, *fa, fbuf_all, layer=l, DB=DB, Ts=Ts, tf=tf)
        fcs[1].append(jnp.stack([jnp.concatenate([n0g, n0v], axis=-1), jnp.concatenate([n1g, n1v], axis=-1)], axis=1))

    y_prompt = xp.reshape(B, T, D)
    y_sample = xs.reshape(Ts, DB, D).transpose(1, 0, 2)
    st = jnp.stack
    return (y_prompt, y_sample, st(ks[0]), st(vs[0]), st(ks[1]), st(vs[1]), st(hs[0]), st(hs[1]),
            st(lcs[0]), st(lcs[1]), st(Ss[0]), st(Ss[1]), st(dcs[0]), st(dcs[1]), st(fcs[0]), st(fcs[1]))
```

```python
import functools

import numpy as np
import jax
import jax.numpy as jnp
from jax import lax
from jax.experimental import pallas as pl
from jax.experimental.pallas import tpu as pltpu

F32 = jnp.float32
BF16 = jnp.bfloat16
EPS = 1e-6

D = 1024
DR = 512
SBW = 512
NH = 8
HD = 64
LRU_C = 8.0
E_IN = 2 * DR + 3 * SBW
DK = 64
DV = 128
KW = NH * DK
VW = NH * DV
CCH = 2 * KW + VW
O_IN = CCH + VW + 2 * NH
O_PAD = CCH + VW + 128
DFF = 4096
FF_SUB = 512
DN_CHUNK = 64
LANES = 128
LOG2E = 1.4426950408889634
MASKED_Z = -1e30
VMEM_LIMIT = 52 * 1024 * 1024


def _cp(*sem):
    return pltpu.CompilerParams(dimension_semantics=sem, vmem_limit_bytes=VMEM_LIMIT)


def _rms(x, w):
    ms = jnp.mean(x * x, axis=-1, keepdims=True)
    return x * lax.rsqrt(ms + EPS) * w


def _dot(a, b):
    return jnp.dot(a.astype(BF16), b.astype(BF16), preferred_element_type=F32)


def _dot_nt(a, b):
    return lax.dot_general(a.astype(BF16), b.astype(BF16), (((1,), (1,)), ((), ())),
                           preferred_element_type=F32)


def _dot_tn(a, b):
    return lax.dot_general(a.astype(BF16), b.astype(BF16), (((0,), (0,)), ((), ())),
                           preferred_element_type=F32)


def _split(a):
    hi = a.astype(BF16)
    lo = (a - hi.astype(F32)).astype(BF16)
    return hi, lo


def _dot2(a, b_exact):
    hi, lo = _split(a)
    b = b_exact.astype(BF16)
    return (jnp.dot(hi, b, preferred_element_type=F32) + jnp.dot(lo, b, preferred_element_type=F32))


def _dot3(a, b):
    ah, al = _split(a)
    bh, bl = _split(b)
    return (jnp.dot(ah, bh, preferred_element_type=F32) + jnp.dot(ah, bl, preferred_element_type=F32)
            + jnp.dot(al, bh, preferred_element_type=F32))


def _expm1(x):
    return jnp.tanh(0.5 * x) * (jnp.exp(x) + 1.0)


def _scan_rows(a, b):
    n = a.shape[0]
    row = lax.broadcasted_iota(jnp.int32, a.shape, 0)
    s = 1
    while s < n:
        m = row >= s
        a_sh = pltpu.roll(a, s, 0)
        b_sh = pltpu.roll(b, s, 0)
        b = jnp.where(m, a * b_sh + b, b)
        a = jnp.where(m, a * a_sh, a)
        s *= 2
    return a, b


def _cumsum_rows(x):
    n = x.shape[0]
    row = lax.broadcasted_iota(jnp.int32, x.shape, 0)
    s = 1
    while s < n:
        x = x + jnp.where(row >= s, pltpu.roll(x, s, 0), 0.0)
        s *= 2
    return x


def _lru_gates(xc, wa, ba, wx, bx, lam):
    r = jax.nn.sigmoid(_dot(xc, wa) + ba)
    gi = jax.nn.sigmoid(_dot(xc, wx) + bx)
    log_a = -LRU_C * r * jax.nn.softplus(-lam)
    a = jnp.exp(log_a)
    mult = jnp.sqrt(-_expm1(2.0 * log_a))
    return a, gi, mult


def _softplus2(z):
    return jnp.maximum(z, 0.0) + jnp.log2(1.0 + jnp.exp2(-jnp.abs(z)))


def _even_in_prompt_kernel(x_ref, nw_ref, win_ref, wkvt_ref, cw_ref, cb_ref, wa_ref, ba_ref, wx_ref, bx_ref,
                           lam_ref, yr_ref, q_ref, kt_ref, vt_ref, k16_ref, v16_ref, hl_ref, tail_ref,
                           xbuf, hcar, *, tm):
    i = pl.program_id(1)
    xn16 = _rms(x_ref[...], nw_ref[...]).astype(BF16)
    proj = jnp.dot(xn16, win_ref[...], preferred_element_type=F32)
    xr = proj[:, 0:DR]
    gr = proj[:, DR:2 * DR]
    q_ref[...] = proj[:, 2 * DR:2 * DR + SBW]
    k16_ref[...] = proj[:, 2 * DR + SBW:2 * DR + 2 * SBW].astype(BF16)
    v16_ref[...] = proj[:, 2 * DR + 2 * SBW:2 * DR + 3 * SBW].astype(BF16)
    kvt = lax.dot_general(wkvt_ref[...], xn16, (((1,), (1,)), ((), ())), preferred_element_type=F32)
    kt_ref[0] = kvt[0:SBW]
    vt_ref[0] = kvt[SBW:2 * SBW]

    @pl.when(i == 0)
    def _():
        xbuf[0:8, :] = jnp.zeros((8, DR), F32)
        hcar[...] = jnp.zeros((1, DR), F32)

    @pl.when(i > 0)
    def _():
        xbuf[0:8, :] = xbuf[tm:tm + 8, :]

    xbuf[8:tm + 8, :] = xr
    cw = cw_ref[...]
    xc = (cw[3:4] * xr + cw[2:3] * xbuf[pl.ds(7, tm), :] + cw[1:2] * xbuf[pl.ds(6, tm), :]
          + cw[0:1] * xbuf[pl.ds(5, tm), :] + cb_ref[...])
    a, gi, mult = _lru_gates(xc, wa_ref[...], ba_ref[...], wx_ref[...], bx_ref[...], lam_ref[...])
    row = lax.broadcasted_iota(jnp.int32, (tm, DR), 0)
    mult = jnp.where(jnp.logical_and(row == 0, i == 0), 1.0, mult)
    b = xc * gi * mult
    pa, hb = _scan_rows(a, b)
    h = hb + pa * hcar[...]
    hcar[...] = h[tm - 1:tm, :]
    yr_ref[...] = h * jax.nn.gelu(gr)
    hl_ref[0] = h[tm - 1:tm, :]
    tail_ref[0] = xbuf[pl.ds(tm + 5, 3), :]


def _even_in_prompt(x, nw, win, wkvt, cw, cb, wa, ba, wx, bx, lam, *, B, T, tm):
    nt = T // tm
    N = B * T
    row = lambda b, i: (b * nt + i, 0)
    const = lambda b, i: (0, 0)
    chan_major = pl.BlockSpec((1, SBW, tm), lambda b, i: (b, 0, i))
    outs = pl.pallas_call(
        functools.partial(_even_in_prompt_kernel, tm=tm),
        grid=(B, nt),
        in_specs=[pl.BlockSpec((tm, D), row), pl.BlockSpec((1, D), const), pl.BlockSpec((D, E_IN), const),
                  pl.BlockSpec((2 * SBW, D), const),
                  pl.BlockSpec((4, DR), const), pl.BlockSpec((1, DR), const), pl.BlockSpec((DR, DR), const),
                  pl.BlockSpec((1, DR), const), pl.BlockSpec((DR, DR), const), pl.BlockSpec((1, DR), const),
                  pl.BlockSpec((1, DR), const)],
        out_specs=[pl.BlockSpec((tm, DR), row), pl.BlockSpec((tm, SBW), row), chan_major, chan_major,
                   pl.BlockSpec((tm, SBW), row), pl.BlockSpec((tm, SBW), row),
                   pl.BlockSpec((1, 1, DR), lambda b, i: (b, 0, 0)),
                   pl.BlockSpec((1, 3, DR), lambda b, i: (b, 0, 0))],
        out_shape=[jax.ShapeDtypeStruct((N, DR), F32), jax.ShapeDtypeStruct((N, SBW), F32),
                   jax.ShapeDtypeStruct((B, SBW, T), F32), jax.ShapeDtypeStruct((B, SBW, T), F32),
                   jax.ShapeDtypeStruct((N, SBW), BF16), jax.ShapeDtypeStruct((N, SBW), BF16),
                   jax.ShapeDtypeStruct((B, 1, DR), F32), jax.ShapeDtypeStruct((B, 3, DR), F32)],
        scratch_shapes=[pltpu.VMEM((tm + 8, DR), F32), pltpu.VMEM((1, DR), F32)],
        compiler_params=_cp("arbitrary", "arbitrary"),
        name="even_in_prompt",
    )(x, nw, win, wkvt, cw, cb, wa, ba, wx, bx, lam)
    return outs


def _even_in_sample_kernel(x_ref, nw_ref, win_ref, cbuf_ref, h0_ref, cw_ref, cb_ref, wa_ref, ba_ref, wx_ref,
                           bx_ref, lam_ref, yr_ref, q_ref, k_ref, v_ref, hl_ref, tail_ref, *, DB, Ts):
    xn = _rms(x_ref[...], nw_ref[...])
    proj = jnp.dot(xn.astype(BF16), win_ref[...], preferred_element_type=F32)
    xr = proj[:, 0:DR]
    gr = proj[:, DR:2 * DR]
    q_ref[...] = proj[:, 2 * DR:2 * DR + SBW]
    k_ref[...] = proj[:, 2 * DR + SBW:2 * DR + 2 * SBW]
    v_ref[...] = proj[:, 2 * DR + 2 * SBW:2 * DR + 3 * SBW]
    xp = [cbuf_ref[0], cbuf_ref[1], cbuf_ref[2]] + [xr[t * DB:(t + 1) * DB] for t in range(Ts)]
    cw = cw_ref[...]
    xc = jnp.concatenate(
        [cw[0:1] * xp[t] + cw[1:2] * xp[t + 1] + cw[2:3] * xp[t + 2] + cw[3:4] * xp[t + 3] + cb_ref[...]
         for t in range(Ts)], axis=0)
    a, gi, mult = _lru_gates(xc, wa_ref[...], ba_ref[...], wx_ref[...], bx_ref[...], lam_ref[...])
    b = xc * gi * mult
    h = h0_ref[...]
    hs = []
    for t in range(Ts):
        h = a[t * DB:(t + 1) * DB] * h + b[t * DB:(t + 1) * DB]
        hs.append(h)
    yr_ref[...] = jnp.concatenate(hs, axis=0) * jax.nn.gelu(gr)
    hl_ref[...] = h
    for r in range(3):
        tail_ref[r] = xp[Ts + r]


def _even_in_sample(x, nw, win, cbuf, h0, cw, cb, wa, ba, wx, bx, lam, *, DB, Ts):
    N = DB * Ts
    return pl.pallas_call(
        functools.partial(_even_in_sample_kernel, DB=DB, Ts=Ts),
        out_shape=[jax.ShapeDtypeStruct((N, DR), F32), jax.ShapeDtypeStruct((N, SBW), F32),
                   jax.ShapeDtypeStruct((N, SBW), F32), jax.ShapeDtypeStruct((N, SBW), F32),
                   jax.ShapeDtypeStruct((DB, DR), F32), jax.ShapeDtypeStruct((3, DB, DR), F32)],
        compiler_params=pltpu.CompilerParams(vmem_limit_bytes=VMEM_LIMIT),
        name="even_in_sample",
    )(x, nw, win, cbuf, h0, cw, cb, wa, ba, wx, bx, lam)


def _attn_prompt_kernel(qi_ref, kb_ref, q_ref, k_ref, v_ref, badd_ref, tri_ref, o_ref, qs, acc, car, *, tq):
    p = pl.program_id(1)
    qi = qi_ref[p]
    kb = kb_ref[p]
    is_diag = kb == qi
    lane = lax.broadcasted_iota(jnp.int32, (tq, LANES), 1)

    @pl.when(is_diag)
    def _():
        acc[...] = jnp.zeros(acc.shape, F32)
        car[...] = jnp.zeros(car.shape, F32)
        for h in range(NH):
            pr, half = h // 2, h % 2
            own = (lane >= HD) if half == 1 else (lane < HD)
            qp = q_ref[:, pr * LANES:(pr + 1) * LANES]
            qs[h] = (jnp.where(own, qp, 0.0) * (LOG2E * HD ** -0.5)).astype(BF16)

    flag = is_diag.astype(jnp.int32)
    tri = tri_ref[...]
    heads = range(NH)

    def scores(h):
        return lax.dot_general(qs[h], k_ref[:, (h // 2) * LANES:(h // 2 + 1) * LANES], (((1,), (1,)), ((), ())),
                               preferred_element_type=F32)

    ahead = 2
    ss = {h: scores(h) for h in range(ahead)}
    zs, sums = [], []
    for h in heads:
        z = ss.pop(h) + badd_ref[flag, h]
        zs.append(z)
        sums.append(jnp.dot(_softplus2(z).astype(BF16), tri, preferred_element_type=F32))
        if h + ahead < NH:
            ss[h + ahead] = scores(h + ahead)
    for h in heads:
        ch = car[h]
        a = jnp.exp2((zs[h] + ch) + sums[h])
        acc[h] += jnp.dot(a.astype(BF16), v_ref[:, (h // 2) * LANES:(h // 2 + 1) * LANES],
                          preferred_element_type=F32)
        car[h] = ch + sums[h][:, 0:1]

    @pl.when(kb == 0)
    def _():
        for pr in range(NH // 2):
            o_ref[:, pr * LANES:(pr + 1) * LANES] = jnp.where(lane < HD, acc[2 * pr], acc[2 * pr + 1])


def _attn_prompt(q, k16, v16, bias, *, B, T, tq):
    nq = T // tq
    qi_tab, kb_tab = [], []
    for qi in range(nq):
        for kb in range(qi, -1, -1):
            qi_tab.append(qi)
            kb_tab.append(kb)
    npairs = len(qi_tab)
    qi_tab = jnp.asarray(np.array(qi_tab, np.int32))
    kb_tab = jnp.asarray(np.array(kb_tab, np.int32))
    causal = np.arange(tq)[None, :] < np.arange(tq)[:, None]
    b2 = (bias.astype(F32) * LOG2E)[:, None, None]
    badd = jnp.stack([jnp.broadcast_to(b2, (NH, tq, tq)), jnp.where(jnp.asarray(causal)[None], b2, MASKED_Z)])
    tri = jnp.asarray(-(np.arange(tq)[:, None] >= np.arange(tq)[None, :]).astype(np.float32), BF16)
    const = lambda *idx: (lambda b, p, qt, kt: idx)
    grid_spec = pltpu.PrefetchScalarGridSpec(
        num_scalar_prefetch=2,
        grid=(B, npairs),
        in_specs=[pl.BlockSpec((tq, SBW), lambda b, p, qt, kt: (b * nq + qt[p], 0)),
                  pl.BlockSpec((tq, SBW), lambda b, p, qt, kt: (b * nq + kt[p], 0)),
                  pl.BlockSpec((tq, SBW), lambda b, p, qt, kt: (b * nq + kt[p], 0)),
                  pl.BlockSpec((2, NH, tq, tq), const(0, 0, 0, 0), pipeline_mode=pl.Buffered(1)),
                  pl.BlockSpec((tq, tq), const(0, 0), pipeline_mode=pl.Buffered(1))],
        out_specs=pl.BlockSpec((tq, SBW), lambda b, p, qt, kt: (b * nq + qt[p], 0)),
        scratch_shapes=[pltpu.VMEM((NH, tq, LANES), BF16), pltpu.VMEM((NH, tq, LANES), F32),
                        pltpu.VMEM((NH, tq, 1), F32)],
    )
    return pl.pallas_call(
        functools.partial(_attn_prompt_kernel, tq=tq),
        grid_spec=grid_spec,
        out_shape=jax.ShapeDtypeStruct((B * T, SBW), F32),
        compiler_params=_cp("arbitrary", "arbitrary"),
        name="attn_prompt",
    )(qi_tab, kb_tab, q, k16, v16, badd, tri)


def _attn_sample_kernel(pt_ref, qrep_ref, kn_ref, vn_ref, bias_ref, tri_ref, *refs, Ts, pps, page):
    kp_refs = refs[0:pps]
    vp_refs = refs[pps:2 * pps]
    o_ref = refs[2 * pps]
    acc, car = refs[2 * pps + 1], refs[2 * pps + 2]
    s = pl.program_id(1)
    R = Ts * NH
    rowh = lax.broadcasted_iota(jnp.int32, (R, SBW), 0)
    lanec = lax.broadcasted_iota(jnp.int32, (R, SBW), 1)
    own = (lanec // HD) == (rowh % NH)
    qe = jnp.where(own, qrep_ref[0], 0.0) * (LOG2E * HD ** -0.5)
    bias = bias_ref[...]

    def sb_terms2(z):
        sp = _softplus2(z)
        return z - sp, -sp

    @pl.when(s == 0)
    def _():
        tq = lax.broadcasted_iota(jnp.int32, (R, 1), 0) // NH
        lbs, lks = [], []
        for j in range(Ts):
            zj = jnp.sum(qe * kn_ref[0, j:j + 1, :], axis=-1, keepdims=True) + bias[:, 0:1]
            lb, lkeep = sb_terms2(zj)
            lbs.append(lb)
            lks.append(jnp.where(tq > j, lkeep, 0.0))
        accv = jnp.zeros((R, SBW), F32)
        later = jnp.zeros((R, 1), F32)
        for j in range(Ts - 1, -1, -1):
            aj = jnp.where(tq > j, jnp.exp2(lbs[j] + later), 0.0)
            accv = accv + aj * vn_ref[0, j:j + 1, :]
            later = later + lks[j]
        acc[...] = accv
        car[...] = later

    qe16 = qe.astype(BF16)
    tri = tri_ref[...]
    zs = [jnp.dot(qe16, kp_refs[r][0, 0].reshape(SBW, page).astype(BF16), preferred_element_type=F32) + bias
          for r in range(pps)]
    sms = [jnp.dot(_softplus2(z).astype(BF16), tri, preferred_element_type=F32) for z in zs]
    run = car[...]
    weights = []
    for z, sm in zip(zs, sms):
        weights.append(jnp.exp2(z + (sm + run)).astype(BF16))
        run = run + sm[:, 0:1]
    total = acc[...]
    for r in range(pps):
        total = total + lax.dot_general(weights[r], vp_refs[r][0, 0].reshape(SBW, page).astype(BF16),
                                        (((1,), (1,)), ((), ())), preferred_element_type=F32)
    acc[...] = total
    car[...] = run

    @pl.when(s == pl.num_programs(1) - 1)
    def _():
        o_ref[0] = jnp.sum(jnp.where(own, acc[...], 0.0).reshape(Ts, NH, SBW), axis=1)


def _attn_sample(q_seq, k_seq, v_seq, bias, pool_k, pool_v, page_table, *, layer, DB, Ts, pps):
    n_pages = page_table.shape[1]
    page = pool_k.shape[-1]
    R = Ts * NH
    nsteps = n_pages // pps
    qrep = jnp.broadcast_to(q_seq[:, :, None, :], (DB, Ts, NH, SBW)).reshape(DB, R, SBW)
    bias_rows = jnp.broadcast_to(jnp.tile(bias.astype(F32) * LOG2E, Ts)[:, None], (R, page))
    tri = jnp.asarray(-(np.arange(page)[:, None] >= np.arange(page)[None, :]).astype(np.float32), BF16)

    def pool_map(r):
        return lambda b, s, pt: (layer, pt[b, n_pages - 1 - (s * pps + r)], 0, 0, 0)

    seq = lambda b, s, pt: (b, 0, 0)
    const = lambda b, s, pt: (0, 0)
    pool_spec = lambda r: pl.BlockSpec((1, 1, NH, HD, page), pool_map(r))
    grid_spec = pltpu.PrefetchScalarGridSpec(
        num_scalar_prefetch=1,
        grid=(DB, nsteps),
        in_specs=([pl.BlockSpec((1, R, SBW), seq), pl.BlockSpec((1, Ts, SBW), seq), pl.BlockSpec((1, Ts, SBW), seq),
                   pl.BlockSpec((R, page), const), pl.BlockSpec((page, page), const)]
                  + [pool_spec(r) for r in range(pps)] + [pool_spec(r) for r in range(pps)]),
        out_specs=pl.BlockSpec((1, Ts, SBW), seq),
        scratch_shapes=[pltpu.VMEM((R, SBW), F32), pltpu.VMEM((R, 1), F32)],
    )
    return pl.pallas_call(
        functools.partial(_attn_sample_kernel, Ts=Ts, pps=pps, page=page),
        grid_spec=grid_spec,
        out_shape=jax.ShapeDtypeStruct((DB, Ts, SBW), F32),
        compiler_params=_cp("arbitrary", "arbitrary"),
        name="attn_sample",
    )(page_table, qrep, k_seq, v_seq, bias_rows, tri, *([pool_k] * pps), *([pool_v] * pps))


def _mix_residual(x_ref, part_refs, wo_ref, nwm_ref):
    off = 0
    y = None
    for pr in part_refs:
        kdim = pr.shape[1]
        t = jnp.dot(pr[...].astype(BF16), wo_ref[off:off + kdim, :], preferred_element_type=F32)
        y = t if y is None else y + t
        off += kdim
    return x_ref[...] + _rms(y, nwm_ref[...])


def _ffn_prompt_kernel(*refs, nparts, tm, tiles_per_seq):
    x_ref = refs[0]
    part_refs = refs[1:1 + nparts]
    (wo_ref, nwm_ref, nwa_ref, nwb_ref, wg_ref, wv_ref, cwg_ref, cwv_ref, cbg_ref, cbv_ref, wd_ref,
     o_ref, tg_ref, tv_ref, xmid, xn, acc, halo_g, halo_v, ubuf) = refs[1 + nparts:]
    i = pl.program_id(0)
    k = pl.program_id(1)

    @pl.when(k == 0)
    def _():
        xm = _mix_residual(x_ref, part_refs, wo_ref, nwm_ref)
        xmid[...] = xm
        xn[...] = _rms(xm, nwa_ref[...]).astype(BF16)
        acc[...] = jnp.zeros(acc.shape, F32)

    seq_start = (i % tiles_per_seq) == 0
    tf = wg_ref.shape[1]
    nsub = tf // FF_SUB
    xnv = xn[...]

    us = []
    for c in range(nsub):
        cs = slice(c * FF_SUB, (c + 1) * FF_SUB)
        for which, (w_ref, halo, t_ref) in enumerate(((wg_ref, halo_g, tg_ref), (wv_ref, halo_v, tv_ref))):
            u = jnp.dot(xnv, w_ref[:, cs], preferred_element_type=F32)
            slot = 2 * c + which
            ubuf[slot, 0:8, :] = jnp.where(seq_start, 0.0, halo[k, :, cs])
            ubuf[slot, 8:tm + 8, :] = u
            halo[k, :, cs] = u[tm - 8:tm, :]
            t_ref[0, :, cs] = u[tm - 2:tm, :]
            us.append(u)

    def conv(slot, cs, cw_ref, cb_ref):
        cw = cw_ref[:, cs]
        return (cw[2:3] * us[slot] + cw[1:2] * ubuf[slot, pl.ds(7, tm), :] + cw[0:1] * ubuf[slot, pl.ds(6, tm), :]
                + cb_ref[:, cs])

    hs = []
    for c in range(nsub):
        cs = slice(c * FF_SUB, (c + 1) * FF_SUB)
        cg = conv(2 * c, cs, cwg_ref, cbg_ref)
        cv = conv(2 * c + 1, cs, cwv_ref, cbv_ref)
        hs.append((jax.nn.gelu(cg) * cv).astype(BF16))
    hmid = jnp.concatenate(hs, axis=1) if nsub > 1 else hs[0]
    acc[...] += jnp.dot(hmid, wd_ref[...], preferred_element_type=F32)

    @pl.when(k == pl.num_programs(1) - 1)
    def _():
        o_ref[...] = xmid[...] + _rms(acc[...], nwb_ref[...])


def _ffn_prompt(x, parts, wout, nwm, nwa, nwb, wup, cw, cb, wdn, *, T, tm, tf):
    N = x.shape[0]
    nk = DFF // tf
    ntiles = N // tm
    row = lambda i, k: (i, 0)
    const = lambda i, k: (0, 0)
    return pl.pallas_call(
        functools.partial(_ffn_prompt_kernel, nparts=len(parts), tm=tm, tiles_per_seq=T // tm),
        grid=(ntiles, nk),
        in_specs=[pl.BlockSpec((tm, D), row)] + [pl.BlockSpec((tm, p.shape[1]), row) for p in parts] + [
                  pl.BlockSpec(wout.shape, const), pl.BlockSpec((1, D), const),
                  pl.BlockSpec((1, D), const), pl.BlockSpec((1, D), const),
                  pl.BlockSpec((D, tf), lambda i, k: (0, k)), pl.BlockSpec((D, tf), lambda i, k: (0, nk + k)),
                  pl.BlockSpec((3, tf), lambda i, k: (0, k)), pl.BlockSpec((3, tf), lambda i, k: (0, nk + k)),
                  pl.BlockSpec((1, tf), lambda i, k: (0, k)), pl.BlockSpec((1, tf), lambda i, k: (0, nk + k)),
                  pl.BlockSpec((tf, D), lambda i, k: (k, 0))],
        out_specs=[pl.BlockSpec((tm, D), row), pl.BlockSpec((1, 2, tf), lambda i, k: (i, 0, k)),
                   pl.BlockSpec((1, 2, tf), lambda i, k: (i, 0, k))],
        out_shape=[jax.ShapeDtypeStruct((N, D), F32), jax.ShapeDtypeStruct((ntiles, 2, DFF), F32),
                   jax.ShapeDtypeStruct((ntiles, 2, DFF), F32)],
        scratch_shapes=[pltpu.VMEM((tm, D), F32), pltpu.VMEM((tm, D), BF16), pltpu.VMEM((tm, D), F32),
                        pltpu.VMEM((nk, 8, tf), F32), pltpu.VMEM((nk, 8, tf), F32),
                        pltpu.VMEM((2 * (tf // FF_SUB), tm + 8, FF_SUB), F32)],
        compiler_params=_cp("arbitrary", "arbitrary"),
        name="ffn_prompt",
    )(x, *parts, wout, nwm, nwa, nwb, wup, wup, cw, cw, cb, cb, wdn)


def _ffn_sample_kernel(*refs, nparts, DB, Ts):
    x_ref = refs[0]
    part_refs = refs[1:1 + nparts]
    (wo_ref, nwm_ref, nwa_ref, nwb_ref, wg_ref, wv_ref, cwg_ref, cwv_ref, cbg_ref, cbv_ref, wd_ref,
     b0g_ref, b0v_ref, b1g_ref, b1v_ref, o_ref, n0g_ref, n0v_ref, n1g_ref, n1v_ref,
     xmid, xn, acc) = refs[1 + nparts:]
    k = pl.program_id(0)

    @pl.when(k == 0)
    def _():
        xm = _mix_residual(x_ref, part_refs, wo_ref, nwm_ref)
        xmid[...] = xm
        xn[...] = _rms(xm, nwa_ref[...]).astype(BF16)
        acc[...] = jnp.zeros(acc.shape, F32)

    def conv(w_ref, cw_ref, cb_ref, b0_ref, b1_ref, n0_ref, n1_ref):
        u = jnp.dot(xn[...], w_ref[...], preferred_element_type=F32)
        xp = [b0_ref[...], b1_ref[...]] + [u[t * DB:(t + 1) * DB] for t in range(Ts)]
        n0_ref[...] = xp[Ts]
        n1_ref[...] = xp[Ts + 1]
        cw = cw_ref[...]
        return jnp.concatenate(
            [cw[0:1] * xp[t] + cw[1:2] * xp[t + 1] + cw[2:3] * xp[t + 2] + cb_ref[...] for t in range(Ts)], axis=0)

    cg = conv(wg_ref, cwg_ref, cbg_ref, b0g_ref, b1g_ref, n0g_ref, n1g_ref)
    cv = conv(wv_ref, cwv_ref, cbv_ref, b0v_ref, b1v_ref, n0v_ref, n1v_ref)
    hmid = jax.nn.gelu(cg) * cv
    acc[...] += jnp.dot(hmid.astype(BF16), wd_ref[...], preferred_element_type=F32)

    @pl.when(k == pl.num_programs(0) - 1)
    def _():
        o_ref[...] = xmid[...] + _rms(acc[...], nwb_ref[...])


def _ffn_sample(x, parts, wout, nwm, nwa, nwb, wup, cw, cb, wdn, fbuf, *, layer, DB, Ts, tf):
    N = DB * Ts
    nk = DFF // tf
    const = lambda k: (0, 0)
    piece = jax.ShapeDtypeStruct((DB, DFF), F32)
    return pl.pallas_call(
        functools.partial(_ffn_sample_kernel, nparts=len(parts), DB=DB, Ts=Ts),
        grid=(nk,),
        in_specs=[pl.BlockSpec((N, D), const)] + [pl.BlockSpec((N, p.shape[1]), const) for p in parts] + [
                  pl.BlockSpec(wout.shape, const), pl.BlockSpec((1, D), const),
                  pl.BlockSpec((1, D), const), pl.BlockSpec((1, D), const),
                  pl.BlockSpec((D, tf), lambda k: (0, k)), pl.BlockSpec((D, tf), lambda k: (0, nk + k)),
                  pl.BlockSpec((3, tf), lambda k: (0, k)), pl.BlockSpec((3, tf), lambda k: (0, nk + k)),
                  pl.BlockSpec((1, tf), lambda k: (0, k)), pl.BlockSpec((1, tf), lambda k: (0, nk + k)),
                  pl.BlockSpec((tf, D), lambda k: (k, 0)),
                  pl.BlockSpec((DB, tf), lambda k: (layer, k)), pl.BlockSpec((DB, tf), lambda k: (layer, nk + k)),
                  pl.BlockSpec((DB, tf), lambda k: (layer, 2 * nk + k)),
                  pl.BlockSpec((DB, tf), lambda k: (layer, 3 * nk + k))],
        out_specs=[pl.BlockSpec((N, D), const)] + [pl.BlockSpec((DB, tf), lambda k: (0, k))] * 4,
        out_shape=[jax.ShapeDtypeStruct((N, D), F32), piece, piece, piece, piece],
        scratch_shapes=[pltpu.VMEM((N, D), F32), pltpu.VMEM((N, D), BF16), pltpu.VMEM((N, D), F32)],
        compiler_params=_cp("arbitrary"),
        name="ffn_sample",
    )(x, *parts, wout, nwm, nwa, nwb, wup, wup, cw, cw, cb, cb, wdn, fbuf, fbuf, fbuf, fbuf)


def _odd_post(c, ba, alog, dtb, g_ref, gt_ref):
    c = c * jax.nn.sigmoid(c)
    q = c[:, 0:KW]
    k = c[:, KW:2 * KW]
    v = c[:, 2 * KW:CCH]
    G = g_ref[...]
    GT = gt_ref[...]
    q = q * _dot2(lax.rsqrt(_dot2(q * q, G) + EPS), GT)
    k = k * _dot2(lax.rsqrt(_dot2(k * k, G) + EPS), GT)
    lane = lax.broadcasted_iota(jnp.int32, ba.shape, 1)
    beta = jax.nn.sigmoid(ba)
    g = -jnp.exp(alog) * jax.nn.softplus(ba + dtb)
    gates = jnp.where(lane < NH, beta, g)
    return q, k, v, gates


def _odd_in_prompt_kernel(x_ref, nw_ref, win_ref, cw_ref, alog_ref, dtb_ref, g_ref, gt_ref,
                          q_ref, k_ref, v_ref, z_ref, gates_ref, tail_ref, cbuf, *, tm):
    i = pl.program_id(1)
    xn = _rms(x_ref[...], nw_ref[...])
    proj = jnp.dot(xn.astype(BF16), win_ref[...], preferred_element_type=F32)
    qkv = proj[:, 0:CCH]
    z_ref[...] = proj[:, CCH:CCH + VW]
    ba = proj[:, CCH + VW:O_PAD]

    @pl.when(i == 0)
    def _():
        cbuf[0:8, :] = jnp.zeros((8, CCH), F32)

    @pl.when(i > 0)
    def _():
        cbuf[0:8, :] = cbuf[tm:tm + 8, :]

    cbuf[8:tm + 8, :] = qkv
    cw = cw_ref[...]
    c = (cw[3:4] * qkv + cw[2:3] * cbuf[pl.ds(7, tm), :] + cw[1:2] * cbuf[pl.ds(6, tm), :]
         + cw[0:1] * cbuf[pl.ds(5, tm), :])
    q, k, v, gates = _odd_post(c, ba, alog_ref[...], dtb_ref[...], g_ref, gt_ref)
    q_ref[...] = q
    k_ref[...] = k
    v_ref[...] = v
    gates_ref[...] = gates
    tail_ref[0] = cbuf[pl.ds(tm + 5, 3), :]


def _odd_in_prompt(x, nw, win, cw, alog, dtb, G, GT, *, B, T, tm):
    nt = T // tm
    N = B * T
    row = lambda b, i: (b * nt + i, 0)
    const = lambda b, i: (0, 0)
    return pl.pallas_call(
        functools.partial(_odd_in_prompt_kernel, tm=tm),
        grid=(B, nt),
        in_specs=[pl.BlockSpec((tm, D), row), pl.BlockSpec((1, D), const), pl.BlockSpec((D, O_PAD), const),
                  pl.BlockSpec((4, CCH), const), pl.BlockSpec((1, LANES), const), pl.BlockSpec((1, LANES), const),
                  pl.BlockSpec((KW, LANES), const), pl.BlockSpec((LANES, KW), const)],
        out_specs=[pl.BlockSpec((tm, KW), row), pl.BlockSpec((tm, KW), row), pl.BlockSpec((tm, VW), row),
                   pl.BlockSpec((tm, VW), row), pl.BlockSpec((tm, LANES), row),
                   pl.BlockSpec((1, 3, CCH), lambda b, i: (b, 0, 0))],
        out_shape=[jax.ShapeDtypeStruct((N, KW), F32), jax.ShapeDtypeStruct((N, KW), F32),
                   jax.ShapeDtypeStruct((N, VW), F32), jax.ShapeDtypeStruct((N, VW), F32),
                   jax.ShapeDtypeStruct((N, LANES), F32), jax.ShapeDtypeStruct((B, 3, CCH), F32)],
        scratch_shapes=[pltpu.VMEM((tm + 8, CCH), F32)],
        compiler_params=_cp("arbitrary", "arbitrary"),
        name="odd_in_prompt",
    )(x, nw, win, cw, alog, dtb, G, GT)


def _odd_in_sample_kernel(x_ref, nw_ref, win_ref, cbuf_ref, cw_ref, alog_ref, dtb_ref, g_ref, gt_ref,
                          q_ref, k_ref, v_ref, z_ref, gates_ref, tail_ref, *, DB, Ts):
    xn = _rms(x_ref[...], nw_ref[...])
    proj = jnp.dot(xn.astype(BF16), win_ref[...], preferred_element_type=F32)
    qkv = proj[:, 0:CCH]
    z_ref[...] = proj[:, CCH:CCH + VW]
    ba = proj[:, CCH + VW:O_PAD]
    xp = [cbuf_ref[0], cbuf_ref[1], cbuf_ref[2]] + [qkv[t * DB:(t + 1) * DB] for t in range(Ts)]
    cw = cw_ref[...]
    c = jnp.concatenate(
        [cw[0:1] * xp[t] + cw[1:2] * xp[t + 1] + cw[2:3] * xp[t + 2] + cw[3:4] * xp[t + 3] for t in range(Ts)],
        axis=0)
    q, k, v, gates = _odd_post(c, ba, alog_ref[...], dtb_ref[...], g_ref, gt_ref)
    q_ref[...] = q
    k_ref[...] = k
    v_ref[...] = v
    gates_ref[...] = gates
    for r in range(3):
        tail_ref[r] = xp[Ts + r]


def _odd_in_sample(x, nw, win, cbuf, cw, alog, dtb, G, GT, *, DB, Ts):
    N = DB * Ts
    return pl.pallas_call(
        functools.partial(_odd_in_sample_kernel, DB=DB, Ts=Ts),
        out_shape=[jax.ShapeDtypeStruct((N, KW), F32), jax.ShapeDtypeStruct((N, KW), F32),
                   jax.ShapeDtypeStruct((N, VW), F32), jax.ShapeDtypeStruct((N, VW), F32),
                   jax.ShapeDtypeStruct((N, LANES), F32), jax.ShapeDtypeStruct((3, DB, CCH), F32)],
        compiler_params=pltpu.CompilerParams(vmem_limit_bytes=VMEM_LIMIT),
        name="odd_in_sample",
    )(x, nw, win, cbuf, cw, alog, dtb, G, GT)


def _delta_kernel(q_ref, k_ref, v_ref, z_ref, gates_ref, s0_ref, nw_ref, o_ref, sout_ref, S, *, C, nb):
    c_idx = pl.program_id(1)

    @pl.when(c_idx == 0)
    def _():
        S[...] = s0_ref[...]

    row = lax.broadcasted_iota(jnp.int32, (C, C), 0)
    col = lax.broadcasted_iota(jnp.int32, (C, C), 1)
    incl = row >= col
    strict = row > col
    nw = nw_ref[...]
    chains = [(b, h) for b in range(nb) for h in range(NH)]
    gates, gcs, gcs_t = [], [], []
    for b in range(nb):
        g = gates_ref[b]
        gc_all = _cumsum_rows(g)
        gpad = jnp.concatenate([gc_all, jnp.zeros((LANES - C, LANES), F32)], axis=0) if C < LANES else gc_all
        gates.append(g)
        gcs.append(gc_all)
        gcs_t.append(gpad.T)
    eye = (row == col).astype(F32)
    qs, kss, gc_, decay_, lms, rhss = [], [], [], [], [], []
    for b, h in chains:
        qh = q_ref[b, :, h * DK:(h + 1) * DK] * (DK ** -0.5)
        kh = k_ref[b, :, h * DK:(h + 1) * DK]
        vh = v_ref[b, :, h * DV:(h + 1) * DV]
        beta = gates[b][:, h:h + 1]
        gc = gcs[b][:, NH + h:NH + h + 1]
        gr = gcs_t[b][NH + h:NH + h + 1, 0:C]
        decay = jnp.where(incl, jnp.exp(jnp.where(incl, gc - gr, 0.0)), 0.0)
        kb = kh * beta
        lm = jnp.where(strict, _dot_nt(kb, kh) * decay, 0.0)
        rhs = jnp.concatenate([vh * beta, kb * jnp.exp(gc)], axis=-1)
        qs.append(qh); kss.append(kh); gc_.append(gc); decay_.append(decay)
        lms.append(lm)
        rhss.append(rhs)
    pws = [-lm for lm in lms]
    invs = [eye + p for p in pws]
    n = 2
    while n < C:
        pws = [_dot(p, p) for p in pws]
        invs = [x + _dot(p, x) for p, x in zip(pws, invs)]
        n *= 2
    sols = [_dot(x, r) for x, r in zip(invs, rhss)]
    for _ in range(2):
        res = [r - (s + _dot3(lm, s)) for r, s, lm in zip(rhss, sols, lms)]
        sols = [s + _dot(x, d) for s, x, d in zip(sols, invs, res)]
    attns = [_dot_nt(qh, kh) * d for qh, kh, d in zip(qs, kss, decay_)]
    Ss = [S[b, h] for b, h in chains]
    v_news = [sol[:, 0:DV] - _dot(sol[:, DV:DV + DK], Sh) for sol, Sh in zip(sols, Ss)]
    os_ = [_dot(qh * jnp.exp(gc), Sh) + _dot(at, vn)
           for qh, gc, Sh, at, vn in zip(qs, gc_, Ss, attns, v_news)]
    for (b, h), kh, gc, Sh, vn, o in zip(chains, kss, gc_, Ss, v_news, os_):
        g_last = gc[C - 1:C, :]
        S[b, h] = Sh * jnp.exp(g_last) + _dot_tn(kh * jnp.exp(g_last - gc), vn)
        zh = z_ref[b, :, h * DV:(h + 1) * DV]
        o_ref[b, :, h * DV:(h + 1) * DV] = _rms(o, nw) * (zh * jax.nn.sigmoid(zh))

    @pl.when(c_idx == pl.num_programs(1) - 1)
    def _():
        sout_ref[...] = S[...]


def _delta(q, k, v, z, gates, s0, nw, *, nseq, nchunks, C, nb, s_base):
    Tq = nchunks * C
    blk = lambda ch: pl.BlockSpec((nb, C, ch), lambda b, c: (b, c, 0))
    return pl.pallas_call(
        functools.partial(_delta_kernel, C=C, nb=nb),
        grid=(nseq // nb, nchunks),
        in_specs=[blk(KW), blk(KW), blk(VW), blk(VW), blk(LANES),
                  pl.BlockSpec((nb, NH, DK, DV), lambda b, c: (s_base // nb + b, 0, 0, 0)),
                  pl.BlockSpec((1, DV), lambda b, c: (0, 0))],
        out_specs=[blk(VW), pl.BlockSpec((nb, NH, DK, DV), lambda b, c: (b, 0, 0, 0))],
        out_shape=[jax.ShapeDtypeStruct((nseq, Tq, VW), F32), jax.ShapeDtypeStruct((nseq, NH, DK, DV), F32)],
        scratch_shapes=[pltpu.VMEM((nb, NH, DK, DV), F32)],
        compiler_params=_cp("arbitrary", "arbitrary"),
        name="delta",
    )(q, k, v, z, gates, s0, nw)


def _tmajor_to_seq(a, DB, Ts):
    return a.reshape(Ts, DB, a.shape[-1]).transpose(1, 0, 2)


def _pad_seq(a, DB, Ts, Tp):
    s = _tmajor_to_seq(a, DB, Ts)
    return jnp.pad(s, ((0, 0), (0, Tp - Ts), (0, 0)))


def kernel(x_prompt, x_sample, cache_sb_k, cache_sb_v, state_lru_h, state_lru_conv, state_dn_S, state_dn_conv,
           state_ffn_conv, page_table, norm_mix_pre, norm_mix_post, norm_ffn_pre, norm_ffn_post, w_in_e,
           lru_conv_w, lru_conv_b, lru_wa, lru_ba, lru_wx, lru_bx, lru_lambda, sb_bias, w_out_e, w_in_o,
           dn_conv_w, dn_A_log, dn_dt_bias, dn_norm_w, w_out_o, ffn_w_up, ffn_conv_w, ffn_conv_b, ffn_w_down):
    B, T, d_model = x_prompt.shape
    DB, Ts, _ = x_sample.shape
    depth = norm_mix_pre.shape[0]
    n_even, n_pool, page = cache_sb_k.shape[0], cache_sb_k.shape[1], cache_sb_k.shape[2]
    n_pages = page_table.shape[1]
    assert d_model == D and w_in_e.shape[-1] == E_IN and w_in_o.shape[-1] == O_IN
    assert ffn_w_down.shape[1] == DFF and n_pages * page > 0 and Ts >= 3
    tm = min(512, T)
    tq = min(256, T)
    tf = 1024
    C = min(DN_CHUNK, T)
    Cs = 8
    pps = next(p for p in (16, 8, 4, 2, 1) if n_pages % p == 0)
    nbp = 2 if B % 2 == 0 else 1
    nbs = next(n for n in (4, 2, 1) if DB % n == 0)
    assert T % tm == 0 and T % tq == 0 and T % C == 0 and Ts <= Cs and (DB * Ts) % 8 == 0

    xp = x_prompt.reshape(B * T, D)
    xs = x_sample.transpose(1, 0, 2).reshape(Ts * DB, D)
    pool_k = cache_sb_k.transpose(0, 1, 3, 4, 2)
    pool_v = cache_sb_v.transpose(0, 1, 3, 4, 2)
    dn_S_all = state_dn_S.reshape(-1, NH, DK, DV)
    fbuf_all = state_ffn_conv.reshape(depth * DB, 2 * 2 * DFF)
    s0_zero = jnp.zeros((B, NH, DK, DV), F32)

    G = jnp.asarray((np.arange(KW)[:, None] // DK == np.arange(LANES)[None, :]).astype(np.float32), BF16)
    GT = G.T

    r1 = lambda a: a.reshape(1, -1)
    ks, vs, hs, lcs, Ss, dcs, fcs = ([[], []] for _ in range(7))
    for l in range(depth):
        j = l // 2
        nw_pre = r1(norm_mix_pre[l])
        if l % 2 == 0:
            win = w_in_e[j].astype(BF16)
            wa = jax.scipy.linalg.block_diag(*lru_wa[j]).astype(BF16)
            wx = jax.scipy.linalg.block_diag(*lru_wx[j]).astype(BF16)
            pe = (lru_conv_w[j], r1(lru_conv_b[j]), wa, r1(lru_ba[j]), wx, r1(lru_bx[j]), r1(lru_lambda[j]))
            wkvt = w_in_e[j][:, 2 * DR + SBW:].T.astype(BF16)
            yr_p, q_p, kt_p, vt_p, k16_p, v16_p, hl_p, lc_p = _even_in_prompt(xp, nw_pre, win, wkvt, *pe,
                                                                              B=B, T=T, tm=tm)
            ya_p = _attn_prompt(q_p, k16_p, v16_p, sb_bias[j], B=B, T=T, tq=tq)
            yr_s, q_s, k_s, v_s, hl_s, lc_s = _even_in_sample(
                xs, nw_pre, win, state_lru_conv[j].transpose(1, 0, 2), state_lru_h[j], *pe, DB=DB, Ts=Ts)
            q_s, k_s, v_s = (_tmajor_to_seq(a, DB, Ts) for a in (q_s, k_s, v_s))
            ya_s = _attn_sample(q_s, k_s, v_s, sb_bias[j], pool_k, pool_v, page_table, layer=j,
                                DB=DB, Ts=Ts, pps=pps)
            ya_s = ya_s.transpose(1, 0, 2).reshape(Ts * DB, SBW)
            wout = w_out_e[j].astype(BF16)
            parts_p, parts_s = [yr_p, ya_p], [yr_s, ya_s]
            tok_major = lambda a: a.reshape(B, NH, HD, T).transpose(0, 3, 1, 2)
            ks[0].append(tok_major(kt_p)); ks[1].append(k_s.reshape(DB, Ts, NH, HD))
            vs[0].append(tok_major(vt_p)); vs[1].append(v_s.reshape(DB, Ts, NH, HD))
            hs[0].append(hl_p.reshape(B, DR)); hs[1].append(hl_s)
            lcs[0].append(lc_p); lcs[1].append(lc_s.transpose(1, 0, 2))
        else:
            win = jnp.pad(w_in_o[j], ((0, 0), (0, O_PAD - O_IN))).astype(BF16)
            alog = jnp.zeros((1, LANES), F32).at[0, NH:2 * NH].set(dn_A_log[j])
            dtb = jnp.zeros((1, LANES), F32).at[0, NH:2 * NH].set(dn_dt_bias[j])
            nwd = r1(dn_norm_w[j])
            q_p, k_p, v_p, z_p, g_p, dc_p = _odd_in_prompt(xp, nw_pre, win, dn_conv_w[j], alog, dtb, G, GT,
                                                           B=B, T=T, tm=tm)
            sq = lambda a: a.reshape(B, T, a.shape[-1])
            o_p, S_p = _delta(sq(q_p), sq(k_p), sq(v_p), sq(z_p), sq(g_p), s0_zero, nwd, nseq=B, nchunks=T // C,
                              C=C, nb=nbp, s_base=0)
            o_p = o_p.reshape(B * T, VW)
            q_s, k_s, v_s, z_s, g_s, dc_s = _odd_in_sample(
                xs, nw_pre, win, state_dn_conv[j].transpose(1, 0, 2), dn_conv_w[j], alog, dtb, G, GT, DB=DB, Ts=Ts)
            q_s, k_s, v_s, z_s, g_s = (_pad_seq(a, DB, Ts, Cs) for a in (q_s, k_s, v_s, z_s, g_s))
            o_s, S_s = _delta(q_s, k_s, v_s, z_s, g_s, dn_S_all, nwd, nseq=DB, nchunks=1, C=Cs, nb=nbs,
                              s_base=j * DB)
            o_s = o_s[:, :Ts].transpose(1, 0, 2).reshape(Ts * DB, VW)
            wout = w_out_o[j].astype(BF16)
            parts_p, parts_s = [o_p], [o_s]
            Ss[0].append(S_p); Ss[1].append(S_s)
            dcs[0].append(dc_p); dcs[1].append(dc_s.transpose(1, 0, 2))
        wup = ffn_w_up[l].astype(BF16)
        wdn = ffn_w_down[l].astype(BF16)
        fa = (wout, r1(norm_mix_post[l]), r1(norm_ffn_pre[l]), r1(norm_ffn_post[l]), wup, ffn_conv_w[l],
              r1(ffn_conv_b[l]), wdn)
        xp, tg, tv = _ffn_prompt(xp, parts_p, *fa, T=T, tm=tm, tf=tf)
        last = (np.arange(B) + 1) * (T // tm) - 1
        fcs[0].append(jnp.concatenate([tg[last], tv[last]], axis=-1))
        xs, n0g, n0v, n1g, n1v = _ffn_sample(xs, parts_s, *fa, fbuf_all, layer=l, DB=DB, Ts=Ts, tf=tf)
        fcs[1].append(jnp.stack([jnp.concatenate([n0g, n0v], axis=-1), jnp.concatenate([n1g, n1v], axis=-1)], axis=1))

    y_prompt = xp.reshape(B, T, D)
    y_sample = xs.reshape(Ts, DB, D).transpose(1, 0, 2)
    st = jnp.stack
    return (y_prompt, y_sample, st(ks[0]), st(vs[0]), st(ks[1]), st(vs[1]), st(hs[0]), st(hs[1]),
            st(lcs[0]), st(lcs[1]), st(Ss[0]), st(Ss[1]), st(dcs[0]), st(dcs[1]), st(fcs[0]), st(fcs[1]))
```

```python
import functools

import numpy as np
import jax
import jax.numpy as jnp
from jax import lax
from jax.experimental import pallas as pl
from jax.experimental.pallas import tpu as pltpu

F32 = jnp.float32
BF16 = jnp.bfloat16
EPS = 1e-6

D = 1024
DR = 512
SBW = 512
NH = 8
HD = 64
LRU_C = 8.0
E_IN = 2 * DR + 3 * SBW
DK = 64
DV = 128
KW = NH * DK
VW = NH * DV
CCH = 2 * KW + VW
O_IN = CCH + VW + 2 * NH
O_PAD = CCH + VW + 128
DFF = 4096
FF_SUB = 512
DN_CHUNK = 64
LANES = 128
LOG2E = 1.4426950408889634
MASKED_Z = -1e30
VMEM_LIMIT = 52 * 1024 * 1024


def _cp(*sem):
    return pltpu.CompilerParams(dimension_semantics=sem, vmem_limit_bytes=VMEM_LIMIT)


def _rms(x, w):
    ms = jnp.mean(x * x, axis=-1, keepdims=True)
    return x * lax.rsqrt(ms + EPS) * w


def _dot(a, b):
    return jnp.dot(a.astype(BF16), b.astype(BF16), preferred_element_type=F32)


def _dot_nt(a, b):
    return lax.dot_general(a.astype(BF16), b.astype(BF16), (((1,), (1,)), ((), ())),
                           preferred_element_type=F32)


def _dot_tn(a, b):
    return lax.dot_general(a.astype(BF16), b.astype(BF16), (((0,), (0,)), ((), ())),
                           preferred_element_type=F32)


def _split(a):
    hi = a.astype(BF16)
    lo = (a - hi.astype(F32)).astype(BF16)
    return hi, lo


def _dot2(a, b_exact):
    hi, lo = _split(a)
    b = b_exact.astype(BF16)
    return (jnp.dot(hi, b, preferred_element_type=F32) + jnp.dot(lo, b, preferred_element_type=F32))


def _dot3(a, b):
    ah, al = _split(a)
    bh, bl = _split(b)
    return (jnp.dot(ah, bh, preferred_element_type=F32) + jnp.dot(ah, bl, preferred_element_type=F32)
            + jnp.dot(al, bh, preferred_element_type=F32))


def _expm1(x):
    return jnp.tanh(0.5 * x) * (jnp.exp(x) + 1.0)


def _scan_rows(a, b, h_in):
    n = a.shape[0]
    sub = lax.broadcasted_iota(jnp.int32, a.shape, 0) % 8
    for s in (1, 2, 4):
        m = sub >= s
        a_sh = pltpu.roll(a, s, 0)
        b_sh = pltpu.roll(b, s, 0)
        b = jnp.where(m, a * b_sh + b, b)
        a = jnp.where(m, a * a_sh, a)
    hs = []
    for j in range(n // 8):
        h = b[j * 8:(j + 1) * 8] + a[j * 8:(j + 1) * 8] * h_in
        hs.append(h)
        h_in = h[7:8]
    return jnp.concatenate(hs, axis=0)


def _cumsum_rows(x):
    n = x.shape[0]
    row = lax.broadcasted_iota(jnp.int32, x.shape, 0)
    s = 1
    while s < n:
        x = x + jnp.where(row >= s, pltpu.roll(x, s, 0), 0.0)
        s *= 2
    return x


def _lru_gates(xc, wa, ba, wx, bx, lam):
    r = jax.nn.sigmoid(_dot(xc, wa) + ba)
    gi = jax.nn.sigmoid(_dot(xc, wx) + bx)
    log_a = -LRU_C * r * jax.nn.softplus(-lam)
    a = jnp.exp(log_a)
    mult = jnp.sqrt(-_expm1(2.0 * log_a))
    return a, gi, mult


def _softplus2(z):
    return jnp.maximum(z, 0.0) + jnp.log2(1.0 + jnp.exp2(-jnp.abs(z)))


def _even_in_prompt_kernel(x_ref, nw_ref, win_ref, wkvt_ref, wk128_ref, kones_ref, cw_ref, cb_ref, wa_ref, ba_ref,
                           wx_ref, bx_ref, lam_ref, yr_ref, q_ref, kt_ref, vt_ref, k16_ref, v16_ref, hl_ref,
                           tail_ref, xbuf, hcar, *, tm):
    i = pl.program_id(1)
    xn16 = _rms(x_ref[...], nw_ref[...]).astype(BF16)
    proj = jnp.dot(xn16, win_ref[...], preferred_element_type=F32)
    xr = proj[:, 0:DR]
    gr = proj[:, DR:2 * DR]
    q_ref[...] = proj[:, 2 * DR:2 * DR + SBW]
    k16_ref[...] = (jnp.dot(xn16, wk128_ref[...], preferred_element_type=F32) + kones_ref[...]).astype(BF16)
    v16_ref[...] = proj[:, 2 * DR + 2 * SBW:2 * DR + 3 * SBW].astype(BF16)
    kvt = lax.dot_general(wkvt_ref[...], xn16, (((1,), (1,)), ((), ())), preferred_element_type=F32)
    kt_ref[0] = kvt[0:SBW]
    vt_ref[0] = kvt[SBW:2 * SBW]

    @pl.when(i == 0)
    def _():
        xbuf[0:8, :] = jnp.zeros((8, DR), F32)
        hcar[...] = jnp.zeros((1, DR), F32)

    @pl.when(i > 0)
    def _():
        xbuf[0:8, :] = xbuf[tm:tm + 8, :]

    xbuf[8:tm + 8, :] = xr
    cw = cw_ref[...]
    xc = (cw[3:4] * xr + cw[2:3] * xbuf[pl.ds(7, tm), :] + cw[1:2] * xbuf[pl.ds(6, tm), :]
          + cw[0:1] * xbuf[pl.ds(5, tm), :] + cb_ref[...])
    a, gi, mult = _lru_gates(xc, wa_ref[...], ba_ref[...], wx_ref[...], bx_ref[...], lam_ref[...])
    row = lax.broadcasted_iota(jnp.int32, (tm, DR), 0)
    mult = jnp.where(jnp.logical_and(row == 0, i == 0), 1.0, mult)
    b = xc * gi * mult
    h = _scan_rows(a, b, hcar[...])
    hcar[...] = h[tm - 1:tm, :]
    yr_ref[...] = h * jax.nn.gelu(gr)
    hl_ref[0] = h[tm - 1:tm, :]
    tail_ref[0] = xbuf[pl.ds(tm + 5, 3), :]


def _even_in_prompt(x, nw, win, wkvt, wk128, kones, cw, cb, wa, ba, wx, bx, lam, *, B, T, tm):
    nt = T // tm
    N = B * T
    row = lambda b, i: (b * nt + i, 0)
    const = lambda b, i: (0, 0)
    chan_major = pl.BlockSpec((1, SBW, tm), lambda b, i: (b, 0, i))
    outs = pl.pallas_call(
        functools.partial(_even_in_prompt_kernel, tm=tm),
        grid=(B, nt),
        in_specs=[pl.BlockSpec((tm, D), row), pl.BlockSpec((1, D), const), pl.BlockSpec((D, E_IN), const),
                  pl.BlockSpec((2 * SBW, D), const), pl.BlockSpec((D, NH * LANES), const),
                  pl.BlockSpec((1, NH * LANES), const),
                  pl.BlockSpec((4, DR), const), pl.BlockSpec((1, DR), const), pl.BlockSpec((DR, DR), const),
                  pl.BlockSpec((1, DR), const), pl.BlockSpec((DR, DR), const), pl.BlockSpec((1, DR), const),
                  pl.BlockSpec((1, DR), const)],
        out_specs=[pl.BlockSpec((tm, DR), row), pl.BlockSpec((tm, SBW), row), chan_major, chan_major,
                   pl.BlockSpec((tm, NH * LANES), row), pl.BlockSpec((tm, SBW), row),
                   pl.BlockSpec((1, 1, DR), lambda b, i: (b, 0, 0)),
                   pl.BlockSpec((1, 3, DR), lambda b, i: (b, 0, 0))],
        out_shape=[jax.ShapeDtypeStruct((N, DR), F32), jax.ShapeDtypeStruct((N, SBW), F32),
                   jax.ShapeDtypeStruct((B, SBW, T), F32), jax.ShapeDtypeStruct((B, SBW, T), F32),
                   jax.ShapeDtypeStruct((N, NH * LANES), BF16), jax.ShapeDtypeStruct((N, SBW), BF16),
                   jax.ShapeDtypeStruct((B, 1, DR), F32), jax.ShapeDtypeStruct((B, 3, DR), F32)],
        scratch_shapes=[pltpu.VMEM((tm + 8, DR), F32), pltpu.VMEM((1, DR), F32)],
        compiler_params=_cp("arbitrary", "arbitrary"),
        name="even_in_prompt",
    )(x, nw, win, wkvt, wk128, kones, cw, cb, wa, ba, wx, bx, lam)
    return outs


def _even_in_sample_kernel(x_ref, nw_ref, win_ref, cbuf_ref, h0_ref, cw_ref, cb_ref, wa_ref, ba_ref, wx_ref,
                           bx_ref, lam_ref, yr_ref, q_ref, k_ref, v_ref, hl_ref, tail_ref, *, DB, Ts):
    xn = _rms(x_ref[...], nw_ref[...])
    proj = jnp.dot(xn.astype(BF16), win_ref[...], preferred_element_type=F32)
    xr = proj[:, 0:DR]
    gr = proj[:, DR:2 * DR]
    q_ref[...] = proj[:, 2 * DR:2 * DR + SBW]
    k_ref[...] = proj[:, 2 * DR + SBW:2 * DR + 2 * SBW]
    v_ref[...] = proj[:, 2 * DR + 2 * SBW:2 * DR + 3 * SBW]
    xp = [cbuf_ref[0], cbuf_ref[1], cbuf_ref[2]] + [xr[t * DB:(t + 1) * DB] for t in range(Ts)]
    cw = cw_ref[...]
    xc = jnp.concatenate(
        [cw[0:1] * xp[t] + cw[1:2] * xp[t + 1] + cw[2:3] * xp[t + 2] + cw[3:4] * xp[t + 3] + cb_ref[...]
         for t in range(Ts)], axis=0)
    a, gi, mult = _lru_gates(xc, wa_ref[...], ba_ref[...], wx_ref[...], bx_ref[...], lam_ref[...])
    b = xc * gi * mult
    h = h0_ref[...]
    hs = []
    for t in range(Ts):
        h = a[t * DB:(t + 1) * DB] * h + b[t * DB:(t + 1) * DB]
        hs.append(h)
    yr_ref[...] = jnp.concatenate(hs, axis=0) * jax.nn.gelu(gr)
    hl_ref[...] = h
    for r in range(3):
        tail_ref[r] = xp[Ts + r]


def _even_in_sample(x, nw, win, cbuf, h0, cw, cb, wa, ba, wx, bx, lam, *, DB, Ts):
    N = DB * Ts
    return pl.pallas_call(
        functools.partial(_even_in_sample_kernel, DB=DB, Ts=Ts),
        out_shape=[jax.ShapeDtypeStruct((N, DR), F32), jax.ShapeDtypeStruct((N, SBW), F32),
                   jax.ShapeDtypeStruct((N, SBW), F32), jax.ShapeDtypeStruct((N, SBW), F32),
                   jax.ShapeDtypeStruct((DB, DR), F32), jax.ShapeDtypeStruct((3, DB, DR), F32)],
        compiler_params=pltpu.CompilerParams(vmem_limit_bytes=VMEM_LIMIT),
        name="even_in_sample",
    )(x, nw, win, cbuf, h0, cw, cb, wa, ba, wx, bx, lam)


def _attn_prompt_kernel(qi_ref, kb_ref, q_ref, k_ref, v_ref, brow_ref, dmask_ref, tri_ref, o_ref, qs, acc, car,
                        *, tq):
    p = pl.program_id(1)
    qi = qi_ref[p]
    kb = kb_ref[p]
    is_diag = kb == qi
    lane = lax.broadcasted_iota(jnp.int32, (tq, LANES), 1)

    @pl.when(is_diag)
    def _():
        acc[...] = jnp.zeros(acc.shape, F32)
        car[...] = jnp.zeros(car.shape, F32)
        for h in range(NH):
            pr, half = h // 2, h % 2
            qp = q_ref[:, pr * LANES:(pr + 1) * LANES]
            qh = pltpu.roll(qp, HD, 1) if half == 1 else qp
            qs[h] = jnp.where(lane < HD, qh * (LOG2E * HD ** -0.5), brow_ref[h:h + 1, :]).astype(BF16)

    tri = tri_ref[...]
    heads = range(NH)

    def scores(h):
        return lax.dot_general(qs[h], k_ref[:, h * LANES:(h + 1) * LANES], (((1,), (1,)), ((), ())),
                               preferred_element_type=F32)

    def sweep(mask):
        ahead = 2
        ss = {h: scores(h) for h in range(ahead)}
        zs, sums = [], []
        for h in heads:
            z = ss.pop(h) if mask is None else ss.pop(h) + mask
            zs.append(z)
            sums.append(jnp.dot(_softplus2(z).astype(BF16), tri, preferred_element_type=F32))
            if h + ahead < NH:
                ss[h + ahead] = scores(h + ahead)
        for h in heads:
            ch = car[h]
            a = jnp.exp2((zs[h] + ch) + sums[h])
            acc[h] += jnp.dot(a.astype(BF16), v_ref[:, (h // 2) * LANES:(h // 2 + 1) * LANES],
                              preferred_element_type=F32)
            car[h] = ch + sums[h][:, 0:1]

    @pl.when(is_diag)
    def _():
        sweep(dmask_ref[...])

    @pl.when(jnp.logical_not(is_diag))
    def _():
        sweep(None)

    @pl.when(kb == 0)
    def _():
        for pr in range(NH // 2):
            o_ref[:, pr * LANES:(pr + 1) * LANES] = jnp.where(lane < HD, acc[2 * pr], acc[2 * pr + 1])


def _attn_prompt(q, k16, v16, bias, *, B, T, tq):
    nq = T // tq
    qi_tab, kb_tab = [], []
    for qi in range(nq):
        for kb in range(qi, -1, -1):
            qi_tab.append(qi)
            kb_tab.append(kb)
    npairs = len(qi_tab)
    qi_tab = jnp.asarray(np.array(qi_tab, np.int32))
    kb_tab = jnp.asarray(np.array(kb_tab, np.int32))
    causal = np.arange(tq)[None, :] < np.arange(tq)[:, None]
    dmask = jnp.asarray(np.where(causal, 0.0, MASKED_Z).astype(np.float32))
    b2 = bias.astype(F32) * LOG2E
    b2_hi = b2.astype(BF16).astype(F32)
    brow = jnp.zeros((NH, LANES), F32).at[:, HD].set(b2_hi).at[:, HD + 1].set(b2 - b2_hi)
    tri = jnp.asarray(-(np.arange(tq)[:, None] >= np.arange(tq)[None, :]).astype(np.float32), BF16)
    const = lambda *idx: (lambda b, p, qt, kt: idx)
    grid_spec = pltpu.PrefetchScalarGridSpec(
        num_scalar_prefetch=2,
        grid=(B, npairs),
        in_specs=[pl.BlockSpec((tq, SBW), lambda b, p, qt, kt: (b * nq + qt[p], 0)),
                  pl.BlockSpec((tq, NH * LANES), lambda b, p, qt, kt: (b * nq + kt[p], 0)),
                  pl.BlockSpec((tq, SBW), lambda b, p, qt, kt: (b * nq + kt[p], 0)),
                  pl.BlockSpec((NH, LANES), const(0, 0)),
                  pl.BlockSpec((tq, tq), const(0, 0), pipeline_mode=pl.Buffered(1)),
                  pl.BlockSpec((tq, tq), const(0, 0), pipeline_mode=pl.Buffered(1))],
        out_specs=pl.BlockSpec((tq, SBW), lambda b, p, qt, kt: (b * nq + qt[p], 0)),
        scratch_shapes=[pltpu.VMEM((NH, tq, LANES), BF16), pltpu.VMEM((NH, tq, LANES), F32),
                        pltpu.VMEM((NH, tq, 1), F32)],
    )
    return pl.pallas_call(
        functools.partial(_attn_prompt_kernel, tq=tq),
        grid_spec=grid_spec,
        out_shape=jax.ShapeDtypeStruct((B * T, SBW), F32),
        compiler_params=_cp("arbitrary", "arbitrary"),
        name="attn_prompt",
    )(qi_tab, kb_tab, q, k16, v16, brow, dmask, tri)


def _attn_sample_kernel(pt_ref, qrep_ref, kn_ref, vn_ref, bias_ref, tri_ref, *refs, Ts, pps, page):
    kp_refs = refs[0:pps]
    vp_refs = refs[pps:2 * pps]
    o_ref = refs[2 * pps]
    acc, car = refs[2 * pps + 1], refs[2 * pps + 2]
    s = pl.program_id(1)
    R = Ts * NH
    rowh = lax.broadcasted_iota(jnp.int32, (R, SBW), 0)
    lanec = lax.broadcasted_iota(jnp.int32, (R, SBW), 1)
    own = (lanec // HD) == (rowh % NH)
    qe = jnp.where(own, qrep_ref[0], 0.0) * (LOG2E * HD ** -0.5)
    bias = bias_ref[...]

    def sb_terms2(z):
        sp = _softplus2(z)
        return z - sp, -sp

    @pl.when(s == 0)
    def _():
        tq = lax.broadcasted_iota(jnp.int32, (R, 1), 0) // NH
        lbs, lks = [], []
        for j in range(Ts):
            zj = jnp.sum(qe * kn_ref[0, j:j + 1, :], axis=-1, keepdims=True) + bias[:, 0:1]
            lb, lkeep = sb_terms2(zj)
            lbs.append(lb)
            lks.append(jnp.where(tq > j, lkeep, 0.0))
        accv = jnp.zeros((R, SBW), F32)
        later = jnp.zeros((R, 1), F32)
        for j in range(Ts - 1, -1, -1):
            aj = jnp.where(tq > j, jnp.exp2(lbs[j] + later), 0.0)
            accv = accv + aj * vn_ref[0, j:j + 1, :]
            later = later + lks[j]
        acc[...] = accv
        car[...] = later

    qe16 = qe.astype(BF16)
    tri = tri_ref[...]
    zs = [jnp.dot(qe16, kp_refs[r][0, 0].reshape(SBW, page).astype(BF16), preferred_element_type=F32) + bias
          for r in range(pps)]
    sms = [jnp.dot(_softplus2(z).astype(BF16), tri, preferred_element_type=F32) for z in zs]
    run = car[...]
    weights = []
    for z, sm in zip(zs, sms):
        weights.append(jnp.exp2(z + (sm + run)).astype(BF16))
        run = run + sm[:, 0:1]
    total = acc[...]
    for r in range(pps):
        total = total + lax.dot_general(weights[r], vp_refs[r][0, 0].reshape(SBW, page).astype(BF16),
                                        (((1,), (1,)), ((), ())), preferred_element_type=F32)
    acc[...] = total
    car[...] = run

    @pl.when(s == pl.num_programs(1) - 1)
    def _():
        o_ref[0] = jnp.sum(jnp.where(own, acc[...], 0.0).reshape(Ts, NH, SBW), axis=1)


def _attn_sample(q_seq, k_seq, v_seq, bias, pool_k, pool_v, page_table, *, layer, DB, Ts, pps):
    n_pages = page_table.shape[1]
    page = pool_k.shape[-1]
    R = Ts * NH
    nsteps = n_pages // pps
    qrep = jnp.broadcast_to(q_seq[:, :, None, :], (DB, Ts, NH, SBW)).reshape(DB, R, SBW)
    bias_rows = jnp.broadcast_to(jnp.tile(bias.astype(F32) * LOG2E, Ts)[:, None], (R, page))
    tri = jnp.asarray(-(np.arange(page)[:, None] >= np.arange(page)[None, :]).astype(np.float32), BF16)

    def pool_map(r):
        return lambda b, s, pt: (layer, pt[b, n_pages - 1 - (s * pps + r)], 0, 0, 0)

    seq = lambda b, s, pt: (b, 0, 0)
    const = lambda b, s, pt: (0, 0)
    pool_spec = lambda r: pl.BlockSpec((1, 1, NH, HD, page), pool_map(r))
    grid_spec = pltpu.PrefetchScalarGridSpec(
        num_scalar_prefetch=1,
        grid=(DB, nsteps),
        in_specs=([pl.BlockSpec((1, R, SBW), seq), pl.BlockSpec((1, Ts, SBW), seq), pl.BlockSpec((1, Ts, SBW), seq),
                   pl.BlockSpec((R, page), const), pl.BlockSpec((page, page), const)]
                  + [pool_spec(r) for r in range(pps)] + [pool_spec(r) for r in range(pps)]),
        out_specs=pl.BlockSpec((1, Ts, SBW), seq),
        scratch_shapes=[pltpu.VMEM((R, SBW), F32), pltpu.VMEM((R, 1), F32)],
    )
    return pl.pallas_call(
        functools.partial(_attn_sample_kernel, Ts=Ts, pps=pps, page=page),
        grid_spec=grid_spec,
        out_shape=jax.ShapeDtypeStruct((DB, Ts, SBW), F32),
        compiler_params=_cp("arbitrary", "arbitrary"),
        name="attn_sample",
    )(page_table, qrep, k_seq, v_seq, bias_rows, tri, *([pool_k] * pps), *([pool_v] * pps))


def _mix_residual(x_ref, part_refs, wo_ref, nwm_ref):
    off = 0
    y = None
    for pr in part_refs:
        kdim = pr.shape[1]
        t = jnp.dot(pr[...].astype(BF16), wo_ref[off:off + kdim, :], preferred_element_type=F32)
        y = t if y is None else y + t
        off += kdim
    return x_ref[...] + _rms(y, nwm_ref[...])


def _ffn_prompt_kernel(*refs, nparts, tm, tiles_per_seq):
    x_ref = refs[0]
    part_refs = refs[1:1 + nparts]
    (wo_ref, nwm_ref, nwa_ref, nwb_ref, wg_ref, wv_ref, cwg_ref, cwv_ref, cbg_ref, cbv_ref, wd_ref,
     o_ref, tg_ref, tv_ref, xmid, xn, acc, halo_g, halo_v, ubuf) = refs[1 + nparts:]
    i = pl.program_id(0)
    k = pl.program_id(1)

    @pl.when(k == 0)
    def _():
        xm = _mix_residual(x_ref, part_refs, wo_ref, nwm_ref)
        xmid[...] = xm
        xn[...] = _rms(xm, nwa_ref[...]).astype(BF16)
        acc[...] = jnp.zeros(acc.shape, F32)

    seq_start = (i % tiles_per_seq) == 0
    tf = wg_ref.shape[1]
    nsub = tf // FF_SUB
    xnv = xn[...]

    us = []
    for c in range(nsub):
        cs = slice(c * FF_SUB, (c + 1) * FF_SUB)
        for which, (w_ref, halo, t_ref) in enumerate(((wg_ref, halo_g, tg_ref), (wv_ref, halo_v, tv_ref))):
            u = jnp.dot(xnv, w_ref[:, cs], preferred_element_type=F32)
            slot = 2 * c + which
            ubuf[slot, 0:8, :] = jnp.where(seq_start, 0.0, halo[k, :, cs])
            ubuf[slot, 8:tm + 8, :] = u
            halo[k, :, cs] = u[tm - 8:tm, :]
            t_ref[0, :, cs] = u[tm - 2:tm, :]
            us.append(u)

    def conv(slot, cs, cw_ref, cb_ref):
        cw = cw_ref[:, cs]
        return (cw[2:3] * us[slot] + cw[1:2] * ubuf[slot, pl.ds(7, tm), :] + cw[0:1] * ubuf[slot, pl.ds(6, tm), :]
                + cb_ref[:, cs])

    hs = []
    for c in range(nsub):
        cs = slice(c * FF_SUB, (c + 1) * FF_SUB)
        cg = conv(2 * c, cs, cwg_ref, cbg_ref)
        cv = conv(2 * c + 1, cs, cwv_ref, cbv_ref)
        hs.append((jax.nn.gelu(cg) * cv).astype(BF16))
    hmid = jnp.concatenate(hs, axis=1) if nsub > 1 else hs[0]
    acc[...] += jnp.dot(hmid, wd_ref[...], preferred_element_type=F32)

    @pl.when(k == pl.num_programs(1) - 1)
    def _():
        o_ref[...] = xmid[...] + _rms(acc[...], nwb_ref[...])


def _ffn_prompt(x, parts, wout, nwm, nwa, nwb, wup, cw, cb, wdn, *, T, tm, tf):
    N = x.shape[0]
    nk = DFF // tf
    ntiles = N // tm
    row = lambda i, k: (i, 0)
    const = lambda i, k: (0, 0)
    return pl.pallas_call(
        functools.partial(_ffn_prompt_kernel, nparts=len(parts), tm=tm, tiles_per_seq=T // tm),
        grid=(ntiles, nk),
        in_specs=[pl.BlockSpec((tm, D), row)] + [pl.BlockSpec((tm, p.shape[1]), row) for p in parts] + [
                  pl.BlockSpec(wout.shape, const), pl.BlockSpec((1, D), const),
                  pl.BlockSpec((1, D), const), pl.BlockSpec((1, D), const),
                  pl.BlockSpec((D, tf), lambda i, k: (0, k)), pl.BlockSpec((D, tf), lambda i, k: (0, nk + k)),
                  pl.BlockSpec((3, tf), lambda i, k: (0, k)), pl.BlockSpec((3, tf), lambda i, k: (0, nk + k)),
                  pl.BlockSpec((1, tf), lambda i, k: (0, k)), pl.BlockSpec((1, tf), lambda i, k: (0, nk + k)),
                  pl.BlockSpec((tf, D), lambda i, k: (k, 0))],
        out_specs=[pl.BlockSpec((tm, D), row), pl.BlockSpec((1, 2, tf), lambda i, k: (i, 0, k)),
                   pl.BlockSpec((1, 2, tf), lambda i, k: (i, 0, k))],
        out_shape=[jax.ShapeDtypeStruct((N, D), F32), jax.ShapeDtypeStruct((ntiles, 2, DFF), F32),
                   jax.ShapeDtypeStruct((ntiles, 2, DFF), F32)],
        scratch_shapes=[pltpu.VMEM((tm, D), F32), pltpu.VMEM((tm, D), BF16), pltpu.VMEM((tm, D), F32),
                        pltpu.VMEM((nk, 8, tf), F32), pltpu.VMEM((nk, 8, tf), F32),
                        pltpu.VMEM((2 * (tf // FF_SUB), tm + 8, FF_SUB), F32)],
        compiler_params=_cp("arbitrary", "arbitrary"),
        name="ffn_prompt",
    )(x, *parts, wout, nwm, nwa, nwb, wup, wup, cw, cw, cb, cb, wdn)


def _ffn_sample_kernel(*refs, nparts, DB, Ts):
    x_ref = refs[0]
    part_refs = refs[1:1 + nparts]
    (wo_ref, nwm_ref, nwa_ref, nwb_ref, wg_ref, wv_ref, cwg_ref, cwv_ref, cbg_ref, cbv_ref, wd_ref,
     b0g_ref, b0v_ref, b1g_ref, b1v_ref, o_ref, n0g_ref, n0v_ref, n1g_ref, n1v_ref,
     xmid, xn, acc) = refs[1 + nparts:]
    k = pl.program_id(0)

    @pl.when(k == 0)
    def _():
        xm = _mix_residual(x_ref, part_refs, wo_ref, nwm_ref)
        xmid[...] = xm
        xn[...] = _rms(xm, nwa_ref[...]).astype(BF16)
        acc[...] = jnp.zeros(acc.shape, F32)

    def conv(w_ref, cw_ref, cb_ref, b0_ref, b1_ref, n0_ref, n1_ref):
        u = jnp.dot(xn[...], w_ref[...], preferred_element_type=F32)
        xp = [b0_ref[...], b1_ref[...]] + [u[t * DB:(t + 1) * DB] for t in range(Ts)]
        n0_ref[...] = xp[Ts]
        n1_ref[...] = xp[Ts + 1]
        cw = cw_ref[...]
        return jnp.concatenate(
            [cw[0:1] * xp[t] + cw[1:2] * xp[t + 1] + cw[2:3] * xp[t + 2] + cb_ref[...] for t in range(Ts)], axis=0)

    cg = conv(wg_ref, cwg_ref, cbg_ref, b0g_ref, b1g_ref, n0g_ref, n1g_ref)
    cv = conv(wv_ref, cwv_ref, cbv_ref, b0v_ref, b1v_ref, n0v_ref, n1v_ref)
    hmid = jax.nn.gelu(cg) * cv
    acc[...] += jnp.dot(hmid.astype(BF16), wd_ref[...], preferred_element_type=F32)

    @pl.when(k == pl.num_programs(0) - 1)
    def _():
        o_ref[...] = xmid[...] + _rms(acc[...], nwb_ref[...])


def _ffn_sample(x, parts, wout, nwm, nwa, nwb, wup, cw, cb, wdn, fbuf, *, layer, DB, Ts, tf):
    N = DB * Ts
    nk = DFF // tf
    const = lambda k: (0, 0)
    piece = jax.ShapeDtypeStruct((DB, DFF), F32)
    return pl.pallas_call(
        functools.partial(_ffn_sample_kernel, nparts=len(parts), DB=DB, Ts=Ts),
        grid=(nk,),
        in_specs=[pl.BlockSpec((N, D), const)] + [pl.BlockSpec((N, p.shape[1]), const) for p in parts] + [
                  pl.BlockSpec(wout.shape, const), pl.BlockSpec((1, D), const),
                  pl.BlockSpec((1, D), const), pl.BlockSpec((1, D), const),
                  pl.BlockSpec((D, tf), lambda k: (0, k)), pl.BlockSpec((D, tf), lambda k: (0, nk + k)),
                  pl.BlockSpec((3, tf), lambda k: (0, k)), pl.BlockSpec((3, tf), lambda k: (0, nk + k)),
                  pl.BlockSpec((1, tf), lambda k: (0, k)), pl.BlockSpec((1, tf), lambda k: (0, nk + k)),
                  pl.BlockSpec((tf, D), lambda k: (k, 0)),
                  pl.BlockSpec((DB, tf), lambda k: (layer, k)), pl.BlockSpec((DB, tf), lambda k: (layer, nk + k)),
                  pl.BlockSpec((DB, tf), lambda k: (layer, 2 * nk + k)),
                  pl.BlockSpec((DB, tf), lambda k: (layer, 3 * nk + k))],
        out_specs=[pl.BlockSpec((N, D), const)] + [pl.BlockSpec((DB, tf), lambda k: (0, k))] * 4,
        out_shape=[jax.ShapeDtypeStruct((N, D), F32), piece, piece, piece, piece],
        scratch_shapes=[pltpu.VMEM((N, D), F32), pltpu.VMEM((N, D), BF16), pltpu.VMEM((N, D), F32)],
        compiler_params=_cp("arbitrary"),
        name="ffn_sample",
    )(x, *parts, wout, nwm, nwa, nwb, wup, wup, cw, cw, cb, cb, wdn, fbuf, fbuf, fbuf, fbuf)


def _odd_post(c, ba, alog, dtb, g_ref, gt_ref):
    c = c * jax.nn.sigmoid(c)
    q = c[:, 0:KW]
    k = c[:, KW:2 * KW]
    v = c[:, 2 * KW:CCH]
    G = g_ref[...]
    GT = gt_ref[...]
    q = q * _dot2(lax.rsqrt(_dot(q * q, G) + EPS), GT)
    k = k * _dot2(lax.rsqrt(_dot(k * k, G) + EPS), GT)
    lane = lax.broadcasted_iota(jnp.int32, ba.shape, 1)
    beta = jax.nn.sigmoid(ba)
    g = -jnp.exp(alog) * jax.nn.softplus(ba + dtb)
    gates = jnp.where(lane < NH, beta, g)
    return q, k, v, gates


def _odd_in_prompt_kernel(x_ref, nw_ref, win_ref, cw_ref, alog_ref, dtb_ref, g_ref, gt_ref,
                          q_ref, k_ref, v_ref, z_ref, gates_ref, tail_ref, cbuf, *, tm):
    i = pl.program_id(1)
    xn = _rms(x_ref[...], nw_ref[...])
    proj = jnp.dot(xn.astype(BF16), win_ref[...], preferred_element_type=F32)
    qkv = proj[:, 0:CCH]
    z_ref[...] = proj[:, CCH:CCH + VW]
    ba = proj[:, CCH + VW:O_PAD]

    @pl.when(i == 0)
    def _():
        cbuf[0:8, :] = jnp.zeros((8, CCH), F32)

    @pl.when(i > 0)
    def _():
        cbuf[0:8, :] = cbuf[tm:tm + 8, :]

    cbuf[8:tm + 8, :] = qkv
    cw = cw_ref[...]
    c = (cw[3:4] * qkv + cw[2:3] * cbuf[pl.ds(7, tm), :] + cw[1:2] * cbuf[pl.ds(6, tm), :]
         + cw[0:1] * cbuf[pl.ds(5, tm), :])
    q, k, v, gates = _odd_post(c, ba, alog_ref[...], dtb_ref[...], g_ref, gt_ref)
    q_ref[...] = q
    k_ref[...] = k
    v_ref[...] = v
    gates_ref[...] = gates
    tail_ref[0] = cbuf[pl.ds(tm + 5, 3), :]


def _odd_in_prompt(x, nw, win, cw, alog, dtb, G, GT, *, B, T, tm):
    nt = T // tm
    N = B * T
    row = lambda b, i: (b * nt + i, 0)
    const = lambda b, i: (0, 0)
    return pl.pallas_call(
        functools.partial(_odd_in_prompt_kernel, tm=tm),
        grid=(B, nt),
        in_specs=[pl.BlockSpec((tm, D), row), pl.BlockSpec((1, D), const), pl.BlockSpec((D, O_PAD), const),
                  pl.BlockSpec((4, CCH), const), pl.BlockSpec((1, LANES), const), pl.BlockSpec((1, LANES), const),
                  pl.BlockSpec((KW, LANES), const), pl.BlockSpec((LANES, KW), const)],
        out_specs=[pl.BlockSpec((tm, KW), row), pl.BlockSpec((tm, KW), row), pl.BlockSpec((tm, VW), row),
                   pl.BlockSpec((tm, VW), row), pl.BlockSpec((tm, LANES), row),
                   pl.BlockSpec((1, 3, CCH), lambda b, i: (b, 0, 0))],
        out_shape=[jax.ShapeDtypeStruct((N, KW), F32), jax.ShapeDtypeStruct((N, KW), F32),
                   jax.ShapeDtypeStruct((N, VW), F32), jax.ShapeDtypeStruct((N, VW), F32),
                   jax.ShapeDtypeStruct((N, LANES), F32), jax.ShapeDtypeStruct((B, 3, CCH), F32)],
        scratch_shapes=[pltpu.VMEM((tm + 8, CCH), F32)],
        compiler_params=_cp("arbitrary", "arbitrary"),
        name="odd_in_prompt",
    )(x, nw, win, cw, alog, dtb, G, GT)


def _odd_in_sample_kernel(x_ref, nw_ref, win_ref, cbuf_ref, cw_ref, alog_ref, dtb_ref, g_ref, gt_ref,
                          q_ref, k_ref, v_ref, z_ref, gates_ref, tail_ref, *, DB, Ts):
    xn = _rms(x_ref[...], nw_ref[...])
    proj = jnp.dot(xn.astype(BF16), win_ref[...], preferred_element_type=F32)
    qkv = proj[:, 0:CCH]
    z_ref[...] = proj[:, CCH:CCH + VW]
    ba = proj[:, CCH + VW:O_PAD]
    xp = [cbuf_ref[0], cbuf_ref[1], cbuf_ref[2]] + [qkv[t * DB:(t + 1) * DB] for t in range(Ts)]
    cw = cw_ref[...]
    c = jnp.concatenate(
        [cw[0:1] * xp[t] + cw[1:2] * xp[t + 1] + cw[2:3] * xp[t + 2] + cw[3:4] * xp[t + 3] for t in range(Ts)],
        axis=0)
    q, k, v, gates = _odd_post(c, ba, alog_ref[...], dtb_ref[...], g_ref, gt_ref)
    q_ref[...] = q
    k_ref[...] = k
    v_ref[...] = v
    gates_ref[...] = gates
    for r in range(3):
        tail_ref[r] = xp[Ts + r]


def _odd_in_sample(x, nw, win, cbuf, cw, alog, dtb, G, GT, *, DB, Ts):
    N = DB * Ts
    return pl.pallas_call(
        functools.partial(_odd_in_sample_kernel, DB=DB, Ts=Ts),
        out_shape=[jax.ShapeDtypeStruct((N, KW), F32), jax.ShapeDtypeStruct((N, KW), F32),
                   jax.ShapeDtypeStruct((N, VW), F32), jax.ShapeDtypeStruct((N, VW), F32),
                   jax.ShapeDtypeStruct((N, LANES), F32), jax.ShapeDtypeStruct((3, DB, CCH), F32)],
        compiler_params=pltpu.CompilerParams(vmem_limit_bytes=VMEM_LIMIT),
        name="odd_in_sample",
    )(x, nw, win, cbuf, cw, alog, dtb, G, GT)


def _delta_kernel(q_ref, k_ref, v_ref, z_ref, gates_ref, s0_ref, nw_ref, o_ref, sout_ref, S, *, C, nb):
    c_idx = pl.program_id(1)

    @pl.when(c_idx == 0)
    def _():
        S[...] = s0_ref[...]

    row = lax.broadcasted_iota(jnp.int32, (C, C), 0)
    col = lax.broadcasted_iota(jnp.int32, (C, C), 1)
    incl = row >= col
    strict = row > col
    nw = nw_ref[...]
    chains = [(b, h) for b in range(nb) for h in range(NH)]
    gates, gcs, gcs_t = [], [], []
    for b in range(nb):
        g = gates_ref[b]
        gc_all = _cumsum_rows(g)
        gpad = jnp.concatenate([gc_all, jnp.zeros((LANES - C, LANES), F32)], axis=0) if C < LANES else gc_all
        gates.append(g)
        gcs.append(gc_all)
        gcs_t.append(gpad.T)
    eye = (row == col).astype(F32)
    qs, kss, gc_, decay_, lms, rhss = [], [], [], [], [], []
    for b, h in chains:
        qh = q_ref[b, :, h * DK:(h + 1) * DK] * (DK ** -0.5)
        kh = k_ref[b, :, h * DK:(h + 1) * DK]
        vh = v_ref[b, :, h * DV:(h + 1) * DV]
        beta = gates[b][:, h:h + 1]
        gc = gcs[b][:, NH + h:NH + h + 1]
        gr = gcs_t[b][NH + h:NH + h + 1, 0:C]
        decay = jnp.where(incl, jnp.exp(jnp.where(incl, gc - gr, 0.0)), 0.0)
        kb = kh * beta
        lm = jnp.where(strict, _dot_nt(kb, kh) * decay, 0.0)
        rhs = jnp.concatenate([vh * beta, kb * jnp.exp(gc)], axis=-1)
        qs.append(qh); kss.append(kh); gc_.append(gc); decay_.append(decay)
        lms.append(lm)
        rhss.append(rhs)
    pws = [-lm for lm in lms]
    invs = [eye + p for p in pws]
    n = 2
    while n < C:
        pws = [_dot(p, p) for p in pws]
        invs = [x + _dot(p, x) for p, x in zip(pws, invs)]
        n *= 2
    sols = [_dot(x, r) for x, r in zip(invs, rhss)]
    for _ in range(2):
        res = [r - (s + _dot3(lm, s)) for r, s, lm in zip(rhss, sols, lms)]
        sols = [s + _dot(x, d) for s, x, d in zip(sols, invs, res)]
    attns = [_dot_nt(qh, kh) * d for qh, kh, d in zip(qs, kss, decay_)]
    Ss = [S[b, h] for b, h in chains]
    v_news = [sol[:, 0:DV] - _dot(sol[:, DV:DV + DK], Sh) for sol, Sh in zip(sols, Ss)]
    os_ = [_dot(qh * jnp.exp(gc), Sh) + _dot(at, vn)
           for qh, gc, Sh, at, vn in zip(qs, gc_, Ss, attns, v_news)]
    for (b, h), kh, gc, Sh, vn, o in zip(chains, kss, gc_, Ss, v_news, os_):
        g_last = gc[C - 1:C, :]
        S[b, h] = Sh * jnp.exp(g_last) + _dot_tn(kh * jnp.exp(g_last - gc), vn)
        zh = z_ref[b, :, h * DV:(h + 1) * DV]
        o_ref[b, :, h * DV:(h + 1) * DV] = _rms(o, nw) * (zh * jax.nn.sigmoid(zh))

    @pl.when(c_idx == pl.num_programs(1) - 1)
    def _():
        sout_ref[...] = S[...]


def _delta(q, k, v, z, gates, s0, nw, *, nseq, nchunks, C, nb, s_base):
    Tq = nchunks * C
    blk = lambda ch: pl.BlockSpec((nb, C, ch), lambda b, c: (b, c, 0))
    return pl.pallas_call(
        functools.partial(_delta_kernel, C=C, nb=nb),
        grid=(nseq // nb, nchunks),
        in_specs=[blk(KW), blk(KW), blk(VW), blk(VW), blk(LANES),
                  pl.BlockSpec((nb, NH, DK, DV), lambda b, c: (s_base // nb + b, 0, 0, 0)),
                  pl.BlockSpec((1, DV), lambda b, c: (0, 0))],
        out_specs=[blk(VW), pl.BlockSpec((nb, NH, DK, DV), lambda b, c: (b, 0, 0, 0))],
        out_shape=[jax.ShapeDtypeStruct((nseq, Tq, VW), F32), jax.ShapeDtypeStruct((nseq, NH, DK, DV), F32)],
        scratch_shapes=[pltpu.VMEM((nb, NH, DK, DV), F32)],
        compiler_params=_cp("arbitrary", "arbitrary"),
        name="delta",
    )(q, k, v, z, gates, s0, nw)


def _tmajor_to_seq(a, DB, Ts):
    return a.reshape(Ts, DB, a.shape[-1]).transpose(1, 0, 2)


def _pad_seq(a, DB, Ts, Tp):
    s = _tmajor_to_seq(a, DB, Ts)
    return jnp.pad(s, ((0, 0), (0, Tp - Ts), (0, 0)))


def kernel(x_prompt, x_sample, cache_sb_k, cache_sb_v, state_lru_h, state_lru_conv, state_dn_S, state_dn_conv,
           state_ffn_conv, page_table, norm_mix_pre, norm_mix_post, norm_ffn_pre, norm_ffn_post, w_in_e,
           lru_conv_w, lru_conv_b, lru_wa, lru_ba, lru_wx, lru_bx, lru_lambda, sb_bias, w_out_e, w_in_o,
           dn_conv_w, dn_A_log, dn_dt_bias, dn_norm_w, w_out_o, ffn_w_up, ffn_conv_w, ffn_conv_b, ffn_w_down):
    B, T, d_model = x_prompt.shape
    DB, Ts, _ = x_sample.shape
    depth = norm_mix_pre.shape[0]
    n_even, n_pool, page = cache_sb_k.shape[0], cache_sb_k.shape[1], cache_sb_k.shape[2]
    n_pages = page_table.shape[1]
    assert d_model == D and w_in_e.shape[-1] == E_IN and w_in_o.shape[-1] == O_IN
    assert ffn_w_down.shape[1] == DFF and n_pages * page > 0 and Ts >= 3
    tm = min(512, T)
    tq = min(256, T)
    tf = 1024
    C = min(DN_CHUNK, T)
    Cs = 8
    pps = next(p for p in (16, 8, 4, 2, 1) if n_pages % p == 0)
    nbp = 2 if B % 2 == 0 else 1
    nbs = next(n for n in (4, 2, 1) if DB % n == 0)
    assert T % tm == 0 and T % tq == 0 and T % C == 0 and Ts <= Cs and (DB * Ts) % 8 == 0

    xp = x_prompt.reshape(B * T, D)
    xs = x_sample.transpose(1, 0, 2).reshape(Ts * DB, D)
    pool_k = cache_sb_k.transpose(0, 1, 3, 4, 2)
    pool_v = cache_sb_v.transpose(0, 1, 3, 4, 2)
    dn_S_all = state_dn_S.reshape(-1, NH, DK, DV)
    fbuf_all = state_ffn_conv.reshape(depth * DB, 2 * 2 * DFF)
    s0_zero = jnp.zeros((B, NH, DK, DV), F32)

    G = jnp.asarray((np.arange(KW)[:, None] // DK == np.arange(LANES)[None, :]).astype(np.float32), BF16)
    GT = G.T

    lane_in_head = np.arange(NH * LANES) % LANES
    kones = jnp.asarray(((lane_in_head == HD) | (lane_in_head == HD + 1)).astype(np.float32)[None, :])

    r1 = lambda a: a.reshape(1, -1)
    ks, vs, hs, lcs, Ss, dcs, fcs = ([[], []] for _ in range(7))
    for l in range(depth):
        j = l // 2
        nw_pre = r1(norm_mix_pre[l])
        if l % 2 == 0:
            win = w_in_e[j].astype(BF16)
            wa = jax.scipy.linalg.block_diag(*lru_wa[j]).astype(BF16)
            wx = jax.scipy.linalg.block_diag(*lru_wx[j]).astype(BF16)
            pe = (lru_conv_w[j], r1(lru_conv_b[j]), wa, r1(lru_ba[j]), wx, r1(lru_bx[j]), r1(lru_lambda[j]))
            wkvt = w_in_e[j][:, 2 * DR + SBW:].T.astype(BF16)
            wk = w_in_e[j][:, 2 * DR + SBW:2 * DR + 2 * SBW].reshape(D, NH, HD)
            wk128 = jnp.pad(wk, ((0, 0), (0, 0), (0, LANES - HD))).reshape(D, NH * LANES).astype(BF16)
            yr_p, q_p, kt_p, vt_p, k16_p, v16_p, hl_p, lc_p = _even_in_prompt(xp, nw_pre, win, wkvt, wk128, kones,
                                                                              *pe, B=B, T=T, tm=tm)
            ya_p = _attn_prompt(q_p, k16_p, v16_p, sb_bias[j], B=B, T=T, tq=tq)
            yr_s, q_s, k_s, v_s, hl_s, lc_s = _even_in_sample(
                xs, nw_pre, win, state_lru_conv[j].transpose(1, 0, 2), state_lru_h[j], *pe, DB=DB, Ts=Ts)
            q_s, k_s, v_s = (_tmajor_to_seq(a, DB, Ts) for a in (q_s, k_s, v_s))
            ya_s = _attn_sample(q_s, k_s, v_s, sb_bias[j], pool_k, pool_v, page_table, layer=j,
                                DB=DB, Ts=Ts, pps=pps)
            ya_s = ya_s.transpose(1, 0, 2).reshape(Ts * DB, SBW)
            wout = w_out_e[j].astype(BF16)
            parts_p, parts_s = [yr_p, ya_p], [yr_s, ya_s]
            tok_major = lambda a: a.reshape(B, NH, HD, T).transpose(0, 3, 1, 2)
            ks[0].append(tok_major(kt_p)); ks[1].append(k_s.reshape(DB, Ts, NH, HD))
            vs[0].append(tok_major(vt_p)); vs[1].append(v_s.reshape(DB, Ts, NH, HD))
            hs[0].append(hl_p.reshape(B, DR)); hs[1].append(hl_s)
            lcs[0].append(lc_p); lcs[1].append(lc_s.transpose(1, 0, 2))
        else:
            win = jnp.pad(w_in_o[j], ((0, 0), (0, O_PAD - O_IN))).astype(BF16)
            alog = jnp.zeros((1, LANES), F32).at[0, NH:2 * NH].set(dn_A_log[j])
            dtb = jnp.zeros((1, LANES), F32).at[0, NH:2 * NH].set(dn_dt_bias[j])
            nwd = r1(dn_norm_w[j])
            q_p, k_p, v_p, z_p, g_p, dc_p = _odd_in_prompt(xp, nw_pre, win, dn_conv_w[j], alog, dtb, G, GT,
                                                           B=B, T=T, tm=tm)
            sq = lambda a: a.reshape(B, T, a.shape[-1])
            o_p, S_p = _delta(sq(q_p), sq(k_p), sq(v_p), sq(z_p), sq(g_p), s0_zero, nwd, nseq=B, nchunks=T // C,
                              C=C, nb=nbp, s_base=0)
            o_p = o_p.reshape(B * T, VW)
            q_s, k_s, v_s, z_s, g_s, dc_s = _odd_in_sample(
                xs, nw_pre, win, state_dn_conv[j].transpose(1, 0, 2), dn_conv_w[j], alog, dtb, G, GT, DB=DB, Ts=Ts)
            q_s, k_s, v_s, z_s, g_s = (_pad_seq(a, DB, Ts, Cs) for a in (q_s, k_s, v_s, z_s, g_s))
            o_s, S_s = _delta(q_s, k_s, v_s, z_s, g_s, dn_S_all, nwd, nseq=DB, nchunks=1, C=Cs, nb=nbs,
                              s_base=j * DB)
            o_s = o_s[:, :Ts].transpose(1, 0, 2).reshape(Ts * DB, VW)
            wout = w_out_o[j].astype(BF16)
            parts_p, parts_s = [o_p], [o_s]
            Ss[0].append(S_p); Ss[1].append(S_s)
            dcs[0].append(dc_p); dcs[1].append(dc_s.transpose(1, 0, 2))
        wup = ffn_w_up[l].astype(BF16)
        wdn = ffn_w_down[l].astype(BF16)
        fa = (wout, r1(norm_mix_post[l]), r1(norm_ffn_pre[l]), r1(norm_ffn_post[l]), wup, ffn_conv_w[l],
              r1(ffn_conv_b[l]), wdn)
        xp, tg, tv = _ffn_prompt(xp, parts_p, *fa, T=T, tm=tm, tf=tf)
        last = (np.arange(B) + 1) * (T // tm) - 1
        fcs[0].append(jnp.concatenate([tg[last], tv[last]], axis=-1))
        xs, n0g, n0v, n1g, n1v = _ffn_sample(xs, parts_s, *fa, fbuf_all, layer=l, DB=DB, Ts=Ts, tf=tf)
        fcs[1].append(jnp.stack([jnp.concatenate([n0g, n0v], axis=-1), jnp.concatenate([n1g, n1v], axis=-1)], axis=1))

    y_prompt = xp.reshape(B, T, D)
    y_sample = xs.reshape(Ts, DB, D).transpose(1, 0, 2)
    st = jnp.stack
    return (y_prompt, y_sample, st(ks[0]), st(vs[0]), st(ks[1]), st(vs[1]), st(hs[0]), st(hs[1]),
            st(lcs[0]), st(lcs[1]), st(Ss[0]), st(Ss[1]), st(dcs[0]), st(dcs[1]), st(fcs[0]), st(fcs[1]))
```

```python
import functools

import numpy as np
import jax
import jax.numpy as jnp
from jax import lax
from jax.experimental import pallas as pl
from jax.experimental.pallas import tpu as pltpu

F32 = jnp.float32
BF16 = jnp.bfloat16
EPS = 1e-6

D = 1024
DR = 512
SBW = 512
NH = 8
HD = 64
LRU_C = 8.0
E_IN = 2 * DR + 3 * SBW
DK = 64
DV = 128
KW = NH * DK
VW = NH * DV
CCH = 2 * KW + VW
O_IN = CCH + VW + 2 * NH
O_PAD = CCH + VW + 128
DFF = 4096
FF_SUB = 512
DN_CHUNK = 64
LANES = 128
LOG2E = 1.4426950408889634
MASKED_Z = -1e30
VMEM_LIMIT = 52 * 1024 * 1024


def _cp(*sem):
    return pltpu.CompilerParams(dimension_semantics=sem, vmem_limit_bytes=VMEM_LIMIT)


def _rms(x, w):
    ms = jnp.mean(x * x, axis=-1, keepdims=True)
    return x * lax.rsqrt(ms + EPS) * w


def _dot(a, b):
    return jnp.dot(a.astype(BF16), b.astype(BF16), preferred_element_type=F32)


def _dot_nt(a, b):
    return lax.dot_general(a.astype(BF16), b.astype(BF16), (((1,), (1,)), ((), ())),
                           preferred_element_type=F32)


def _dot_tn(a, b):
    return lax.dot_general(a.astype(BF16), b.astype(BF16), (((0,), (0,)), ((), ())),
                           preferred_element_type=F32)


def _split(a):
    hi = a.astype(BF16)
    lo = (a - hi.astype(F32)).astype(BF16)
    return hi, lo


def _dot2(a, b_exact):
    hi, lo = _split(a)
    b = b_exact.astype(BF16)
    return (jnp.dot(hi, b, preferred_element_type=F32) + jnp.dot(lo, b, preferred_element_type=F32))


def _dot3(a, b):
    ah, al = _split(a)
    bh, bl = _split(b)
    return (jnp.dot(ah, bh, preferred_element_type=F32) + jnp.dot(ah, bl, preferred_element_type=F32)
            + jnp.dot(al, bh, preferred_element_type=F32))


def _expm1(x):
    return jnp.tanh(0.5 * x) * (jnp.exp(x) + 1.0)


def _scan_rows(a, b, h_in):
    n = a.shape[0]
    sub = lax.broadcasted_iota(jnp.int32, a.shape, 0) % 8
    for s in (1, 2, 4):
        m = sub >= s
        a_sh = pltpu.roll(a, s, 0)
        b_sh = pltpu.roll(b, s, 0)
        b = jnp.where(m, a * b_sh + b, b)
        a = jnp.where(m, a * a_sh, a)
    hs = []
    for j in range(n // 8):
        h = b[j * 8:(j + 1) * 8] + a[j * 8:(j + 1) * 8] * h_in
        hs.append(h)
        h_in = h[7:8]
    return jnp.concatenate(hs, axis=0)


def _cumsum_rows(x):
    n = x.shape[0]
    row = lax.broadcasted_iota(jnp.int32, x.shape, 0)
    s = 1
    while s < n:
        x = x + jnp.where(row >= s, pltpu.roll(x, s, 0), 0.0)
        s *= 2
    return x


def _lru_gates(xc, wa, ba, wx, bx, lam):
    r = jax.nn.sigmoid(_dot(xc, wa) + ba)
    gi = jax.nn.sigmoid(_dot(xc, wx) + bx)
    log_a = -LRU_C * r * jax.nn.softplus(-lam)
    a = jnp.exp(log_a)
    mult = jnp.sqrt(-_expm1(2.0 * log_a))
    return a, gi, mult


def _softplus2(z):
    return jnp.maximum(z, 0.0) + jnp.log2(1.0 + jnp.exp2(-jnp.abs(z)))


def _even_in_prompt_kernel(x_ref, nw_ref, win_ref, wkvt_ref, wk128_ref, kones_ref, cw_ref, cb_ref, wa_ref, ba_ref,
                           wx_ref, bx_ref, lam_ref, yr_ref, q_ref, kt_ref, vt_ref, k16_ref, v16_ref, hl_ref,
                           tail_ref, xbuf, hcar, *, tm):
    i = pl.program_id(1)
    xn16 = _rms(x_ref[...], nw_ref[...]).astype(BF16)
    proj = jnp.dot(xn16, win_ref[...], preferred_element_type=F32)
    xr = proj[:, 0:DR]
    gr = proj[:, DR:2 * DR]
    q_ref[...] = proj[:, 2 * DR:2 * DR + SBW]
    k16_ref[...] = (jnp.dot(xn16, wk128_ref[...], preferred_element_type=F32) + kones_ref[...]).astype(BF16)
    v16_ref[...] = proj[:, 2 * DR + 2 * SBW:2 * DR + 3 * SBW].astype(BF16)
    kvt = lax.dot_general(wkvt_ref[...], xn16, (((1,), (1,)), ((), ())), preferred_element_type=F32)
    kt_ref[0] = kvt[0:SBW]
    vt_ref[0] = kvt[SBW:2 * SBW]

    @pl.when(i == 0)
    def _():
        xbuf[0:8, :] = jnp.zeros((8, DR), F32)
        hcar[...] = jnp.zeros((1, DR), F32)

    @pl.when(i > 0)
    def _():
        xbuf[0:8, :] = xbuf[tm:tm + 8, :]

    xbuf[8:tm + 8, :] = xr
    cw = cw_ref[...]
    xc = (cw[3:4] * xr + cw[2:3] * xbuf[pl.ds(7, tm), :] + cw[1:2] * xbuf[pl.ds(6, tm), :]
          + cw[0:1] * xbuf[pl.ds(5, tm), :] + cb_ref[...])
    a, gi, mult = _lru_gates(xc, wa_ref[...], ba_ref[...], wx_ref[...], bx_ref[...], lam_ref[...])
    row = lax.broadcasted_iota(jnp.int32, (tm, DR), 0)
    mult = jnp.where(jnp.logical_and(row == 0, i == 0), 1.0, mult)
    b = xc * gi * mult
    h = _scan_rows(a, b, hcar[...])
    hcar[...] = h[tm - 1:tm, :]
    yr_ref[...] = h * jax.nn.gelu(gr)
    hl_ref[0] = h[tm - 1:tm, :]
    tail_ref[0] = xbuf[pl.ds(tm + 5, 3), :]


def _even_in_prompt(x, nw, win, wkvt, wk128, kones, cw, cb, wa, ba, wx, bx, lam, *, B, T, tm):
    nt = T // tm
    N = B * T
    row = lambda b, i: (b * nt + i, 0)
    const = lambda b, i: (0, 0)
    chan_major = pl.BlockSpec((1, SBW, tm), lambda b, i: (b, 0, i))
    outs = pl.pallas_call(
        functools.partial(_even_in_prompt_kernel, tm=tm),
        grid=(B, nt),
        in_specs=[pl.BlockSpec((tm, D), row), pl.BlockSpec((1, D), const), pl.BlockSpec((D, E_IN), const),
                  pl.BlockSpec((2 * SBW, D), const), pl.BlockSpec((D, NH * LANES), const),
                  pl.BlockSpec((1, NH * LANES), const),
                  pl.BlockSpec((4, DR), const), pl.BlockSpec((1, DR), const), pl.BlockSpec((DR, DR), const),
                  pl.BlockSpec((1, DR), const), pl.BlockSpec((DR, DR), const), pl.BlockSpec((1, DR), const),
                  pl.BlockSpec((1, DR), const)],
        out_specs=[pl.BlockSpec((tm, DR), row), pl.BlockSpec((tm, SBW), row), chan_major, chan_major,
                   pl.BlockSpec((tm, NH * LANES), row), pl.BlockSpec((tm, SBW), row),
                   pl.BlockSpec((1, 1, DR), lambda b, i: (b, 0, 0)),
                   pl.BlockSpec((1, 3, DR), lambda b, i: (b, 0, 0))],
        out_shape=[jax.ShapeDtypeStruct((N, DR), F32), jax.ShapeDtypeStruct((N, SBW), F32),
                   jax.ShapeDtypeStruct((B, SBW, T), F32), jax.ShapeDtypeStruct((B, SBW, T), F32),
                   jax.ShapeDtypeStruct((N, NH * LANES), BF16), jax.ShapeDtypeStruct((N, SBW), BF16),
                   jax.ShapeDtypeStruct((B, 1, DR), F32), jax.ShapeDtypeStruct((B, 3, DR), F32)],
        scratch_shapes=[pltpu.VMEM((tm + 8, DR), F32), pltpu.VMEM((1, DR), F32)],
        compiler_params=_cp("arbitrary", "arbitrary"),
        name="even_in_prompt",
    )(x, nw, win, wkvt, wk128, kones, cw, cb, wa, ba, wx, bx, lam)
    return outs


def _even_in_sample_kernel(x_ref, nw_ref, win_ref, cbuf_ref, h0_ref, cw_ref, cb_ref, wa_ref, ba_ref, wx_ref,
                           bx_ref, lam_ref, yr_ref, q_ref, k_ref, v_ref, hl_ref, tail_ref, *, DB, Ts):
    xn = _rms(x_ref[...], nw_ref[...])
    proj = jnp.dot(xn.astype(BF16), win_ref[...], preferred_element_type=F32)
    xr = proj[:, 0:DR]
    gr = proj[:, DR:2 * DR]
    q_ref[...] = proj[:, 2 * DR:2 * DR + SBW]
    k_ref[...] = proj[:, 2 * DR + SBW:2 * DR + 2 * SBW]
    v_ref[...] = proj[:, 2 * DR + 2 * SBW:2 * DR + 3 * SBW]
    xp = [cbuf_ref[0], cbuf_ref[1], cbuf_ref[2]] + [xr[t * DB:(t + 1) * DB] for t in range(Ts)]
    cw = cw_ref[...]
    xc = jnp.concatenate(
        [cw[0:1] * xp[t] + cw[1:2] * xp[t + 1] + cw[2:3] * xp[t + 2] + cw[3:4] * xp[t + 3] + cb_ref[...]
         for t in range(Ts)], axis=0)
    a, gi, mult = _lru_gates(xc, wa_ref[...], ba_ref[...], wx_ref[...], bx_ref[...], lam_ref[...])
    b = xc * gi * mult
    h = h0_ref[...]
    hs = []
    for t in range(Ts):
        h = a[t * DB:(t + 1) * DB] * h + b[t * DB:(t + 1) * DB]
        hs.append(h)
    yr_ref[...] = jnp.concatenate(hs, axis=0) * jax.nn.gelu(gr)
    hl_ref[...] = h
    for r in range(3):
        tail_ref[r] = xp[Ts + r]


def _even_in_sample(x, nw, win, cbuf, h0, cw, cb, wa, ba, wx, bx, lam, *, DB, Ts):
    N = DB * Ts
    return pl.pallas_call(
        functools.partial(_even_in_sample_kernel, DB=DB, Ts=Ts),
        out_shape=[jax.ShapeDtypeStruct((N, DR), F32), jax.ShapeDtypeStruct((N, SBW), F32),
                   jax.ShapeDtypeStruct((N, SBW), F32), jax.ShapeDtypeStruct((N, SBW), F32),
                   jax.ShapeDtypeStruct((DB, DR), F32), jax.ShapeDtypeStruct((3, DB, DR), F32)],
        compiler_params=pltpu.CompilerParams(vmem_limit_bytes=VMEM_LIMIT),
        name="even_in_sample",
    )(x, nw, win, cbuf, h0, cw, cb, wa, ba, wx, bx, lam)


def _attn_prompt_kernel(qi_ref, kb_ref, q_ref, k_ref, v_ref, brow_ref, dmask_ref, tri_ref, o_ref, qs, acc, car,
                        *, tq, kpb):
    p = pl.program_id(1)
    qi = qi_ref[p]
    kb = kb_ref[p]
    lane = lax.broadcasted_iota(jnp.int32, (tq, LANES), 1)

    @pl.when(kb == qi // kpb)
    def _():
        acc[...] = jnp.zeros(acc.shape, F32)
        car[...] = jnp.zeros(car.shape, F32)
        for h in range(NH):
            pr, half = h // 2, h % 2
            qp = q_ref[:, pr * LANES:(pr + 1) * LANES]
            qh = pltpu.roll(qp, HD, 1) if half == 1 else qp
            qs[h] = jnp.where(lane < HD, qh * (LOG2E * HD ** -0.5), brow_ref[h:h + 1, :]).astype(BF16)

    tri = tri_ref[...]
    heads = range(NH)

    def sweep(mask, rows):
        def scores(h):
            return lax.dot_general(qs[h], k_ref[rows, h * LANES:(h + 1) * LANES], (((1,), (1,)), ((), ())),
                                   preferred_element_type=F32)

        ahead = 2
        ss = {h: scores(h) for h in range(ahead)}
        zs, sums = [], []
        for h in heads:
            z = ss.pop(h) if mask is None else ss.pop(h) + mask
            zs.append(z)
            sums.append(jnp.dot(_softplus2(z).astype(BF16), tri, preferred_element_type=F32))
            if h + ahead < NH:
                ss[h + ahead] = scores(h + ahead)
        for h in heads:
            ch = car[h]
            a = jnp.exp2((zs[h] + ch) + sums[h])
            acc[h] += jnp.dot(a.astype(BF16), v_ref[rows, (h // 2) * LANES:(h // 2 + 1) * LANES],
                              preferred_element_type=F32)
            car[h] = ch + sums[h][:, 0:1]

    for sub in reversed(range(kpb)):
        blk = kb * kpb + sub
        rows = slice(sub * tq, (sub + 1) * tq)

        @pl.when(blk == qi)
        def _():
            sweep(dmask_ref[...], rows)

        @pl.when(blk < qi)
        def _():
            sweep(None, rows)

    @pl.when(kb == 0)
    def _():
        for pr in range(NH // 2):
            o_ref[:, pr * LANES:(pr + 1) * LANES] = jnp.where(lane < HD, acc[2 * pr], acc[2 * pr + 1])


def _attn_prompt(q, k16, v16, bias, *, B, T, tq):
    nq = T // tq
    kpb = 2 if nq % 2 == 0 else 1
    nkg = nq // kpb
    qi_tab, kb_tab = [], []
    for qi in range(nq):
        for kb in range(qi // kpb, -1, -1):
            qi_tab.append(qi)
            kb_tab.append(kb)
    npairs = len(qi_tab)
    qi_tab = jnp.asarray(np.array(qi_tab, np.int32))
    kb_tab = jnp.asarray(np.array(kb_tab, np.int32))
    causal = np.arange(tq)[None, :] < np.arange(tq)[:, None]
    dmask = jnp.asarray(np.where(causal, 0.0, MASKED_Z).astype(np.float32))
    b2 = bias.astype(F32) * LOG2E
    b2_hi = b2.astype(BF16).astype(F32)
    brow = jnp.zeros((NH, LANES), F32).at[:, HD].set(b2_hi).at[:, HD + 1].set(b2 - b2_hi)
    tri = jnp.asarray(-(np.arange(tq)[:, None] >= np.arange(tq)[None, :]).astype(np.float32), BF16)
    const = lambda *idx: (lambda b, p, qt, kt: idx)
    grid_spec = pltpu.PrefetchScalarGridSpec(
        num_scalar_prefetch=2,
        grid=(B, npairs),
        in_specs=[pl.BlockSpec((tq, SBW), lambda b, p, qt, kt: (b * nq + qt[p], 0)),
                  pl.BlockSpec((kpb * tq, NH * LANES), lambda b, p, qt, kt: (b * nkg + kt[p], 0)),
                  pl.BlockSpec((kpb * tq, SBW), lambda b, p, qt, kt: (b * nkg + kt[p], 0)),
                  pl.BlockSpec((NH, LANES), const(0, 0)),
                  pl.BlockSpec((tq, tq), const(0, 0), pipeline_mode=pl.Buffered(1)),
                  pl.BlockSpec((tq, tq), const(0, 0), pipeline_mode=pl.Buffered(1))],
        out_specs=pl.BlockSpec((tq, SBW), lambda b, p, qt, kt: (b * nq + qt[p], 0)),
        scratch_shapes=[pltpu.VMEM((NH, tq, LANES), BF16), pltpu.VMEM((NH, tq, LANES), F32),
                        pltpu.VMEM((NH, tq, 1), F32)],
    )
    return pl.pallas_call(
        functools.partial(_attn_prompt_kernel, tq=tq, kpb=kpb),
        grid_spec=grid_spec,
        out_shape=jax.ShapeDtypeStruct((B * T, SBW), F32),
        compiler_params=_cp("arbitrary", "arbitrary"),
        name="attn_prompt",
    )(qi_tab, kb_tab, q, k16, v16, brow, dmask, tri)


def _attn_sample_kernel(pt_ref, qrep_ref, kn_ref, vn_ref, bias_ref, tri_ref, *refs, Ts, pps, page):
    kp_refs = refs[0:pps]
    vp_refs = refs[pps:2 * pps]
    o_ref = refs[2 * pps]
    acc, car = refs[2 * pps + 1], refs[2 * pps + 2]
    s = pl.program_id(1)
    R = Ts * NH
    rowh = lax.broadcasted_iota(jnp.int32, (R, SBW), 0)
    lanec = lax.broadcasted_iota(jnp.int32, (R, SBW), 1)
    own = (lanec // HD) == (rowh % NH)
    qe = jnp.where(own, qrep_ref[0], 0.0) * (LOG2E * HD ** -0.5)
    bias = bias_ref[...]

    def sb_terms2(z):
        sp = _softplus2(z)
        return z - sp, -sp

    @pl.when(s == 0)
    def _():
        tq = lax.broadcasted_iota(jnp.int32, (R, 1), 0) // NH
        lbs, lks = [], []
        for j in range(Ts):
            zj = jnp.sum(qe * kn_ref[0, j:j + 1, :], axis=-1, keepdims=True) + bias[:, 0:1]
            lb, lkeep = sb_terms2(zj)
            lbs.append(lb)
            lks.append(jnp.where(tq > j, lkeep, 0.0))
        accv = jnp.zeros((R, SBW), F32)
        later = jnp.zeros((R, 1), F32)
        for j in range(Ts - 1, -1, -1):
            aj = jnp.where(tq > j, jnp.exp2(lbs[j] + later), 0.0)
            accv = accv + aj * vn_ref[0, j:j + 1, :]
            later = later + lks[j]
        acc[...] = accv
        car[...] = later

    qe16 = qe.astype(BF16)
    tri = tri_ref[...]
    zs = [jnp.dot(qe16, kp_refs[r][0, 0].reshape(SBW, page).astype(BF16), preferred_element_type=F32) + bias
          for r in range(pps)]
    sms = [jnp.dot(_softplus2(z).astype(BF16), tri, preferred_element_type=F32) for z in zs]
    run = car[...]
    weights = []
    for z, sm in zip(zs, sms):
        weights.append(jnp.exp2(z + (sm + run)).astype(BF16))
        run = run + sm[:, 0:1]
    total = acc[...]
    for r in range(pps):
        total = total + lax.dot_general(weights[r], vp_refs[r][0, 0].reshape(SBW, page).astype(BF16),
                                        (((1,), (1,)), ((), ())), preferred_element_type=F32)
    acc[...] = total
    car[...] = run

    @pl.when(s == pl.num_programs(1) - 1)
    def _():
        o_ref[0] = jnp.sum(jnp.where(own, acc[...], 0.0).reshape(Ts, NH, SBW), axis=1)


def _attn_sample(q_seq, k_seq, v_seq, bias, pool_k, pool_v, page_table, *, layer, DB, Ts, pps):
    n_pages = page_table.shape[1]
    page = pool_k.shape[-1]
    R = Ts * NH
    nsteps = n_pages // pps
    qrep = jnp.broadcast_to(q_seq[:, :, None, :], (DB, Ts, NH, SBW)).reshape(DB, R, SBW)
    bias_rows = jnp.broadcast_to(jnp.tile(bias.astype(F32) * LOG2E, Ts)[:, None], (R, page))
    tri = jnp.asarray(-(np.arange(page)[:, None] >= np.arange(page)[None, :]).astype(np.float32), BF16)

    def pool_map(r):
        return lambda b, s, pt: (layer, pt[b, n_pages - 1 - (s * pps + r)], 0, 0, 0)

    seq = lambda b, s, pt: (b, 0, 0)
    const = lambda b, s, pt: (0, 0)
    pool_spec = lambda r: pl.BlockSpec((1, 1, NH, HD, page), pool_map(r))
    grid_spec = pltpu.PrefetchScalarGridSpec(
        num_scalar_prefetch=1,
        grid=(DB, nsteps),
        in_specs=([pl.BlockSpec((1, R, SBW), seq), pl.BlockSpec((1, Ts, SBW), seq), pl.BlockSpec((1, Ts, SBW), seq),
                   pl.BlockSpec((R, page), const), pl.BlockSpec((page, page), const)]
                  + [pool_spec(r) for r in range(pps)] + [pool_spec(r) for r in range(pps)]),
        out_specs=pl.BlockSpec((1, Ts, SBW), seq),
        scratch_shapes=[pltpu.VMEM((R, SBW), F32), pltpu.VMEM((R, 1), F32)],
    )
    return pl.pallas_call(
        functools.partial(_attn_sample_kernel, Ts=Ts, pps=pps, page=page),
        grid_spec=grid_spec,
        out_shape=jax.ShapeDtypeStruct((DB, Ts, SBW), F32),
        compiler_params=_cp("arbitrary", "arbitrary"),
        name="attn_sample",
    )(page_table, qrep, k_seq, v_seq, bias_rows, tri, *([pool_k] * pps), *([pool_v] * pps))


def _mix_residual(x_ref, part_refs, wo_ref, nwm_ref):
    off = 0
    y = None
    for pr in part_refs:
        kdim = pr.shape[1]
        t = jnp.dot(pr[...].astype(BF16), wo_ref[off:off + kdim, :], preferred_element_type=F32)
        y = t if y is None else y + t
        off += kdim
    return x_ref[...] + _rms(y, nwm_ref[...])


def _ffn_prompt_kernel(*refs, nparts, tm, tiles_per_seq):
    x_ref = refs[0]
    part_refs = refs[1:1 + nparts]
    (wo_ref, nwm_ref, nwa_ref, nwb_ref, wg_ref, wv_ref, cwg_ref, cwv_ref, cbg_ref, cbv_ref, wd_ref,
     o_ref, tg_ref, tv_ref, xmid, xn, acc, halo_g, halo_v, ubuf) = refs[1 + nparts:]
    i = pl.program_id(0)
    k = pl.program_id(1)

    @pl.when(k == 0)
    def _():
        xm = _mix_residual(x_ref, part_refs, wo_ref, nwm_ref)
        xmid[...] = xm
        xn[...] = _rms(xm, nwa_ref[...]).astype(BF16)
        acc[...] = jnp.zeros(acc.shape, F32)

    seq_start = (i % tiles_per_seq) == 0
    tf = wg_ref.shape[1]
    nsub = tf // FF_SUB
    xnv = xn[...]

    us = []
    for c in range(nsub):
        cs = slice(c * FF_SUB, (c + 1) * FF_SUB)
        for which, (w_ref, halo, t_ref) in enumerate(((wg_ref, halo_g, tg_ref), (wv_ref, halo_v, tv_ref))):
            u = jnp.dot(xnv, w_ref[:, cs], preferred_element_type=F32)
            slot = 2 * c + which
            ubuf[slot, 0:8, :] = jnp.where(seq_start, 0.0, halo[k, :, cs])
            ubuf[slot, 8:tm + 8, :] = u
            halo[k, :, cs] = u[tm - 8:tm, :]
            t_ref[0, :, cs] = u[tm - 2:tm, :]
            us.append(u)

    def conv(slot, cs, cw_ref, cb_ref):
        cw = cw_ref[:, cs]
        return (cw[2:3] * us[slot] + cw[1:2] * ubuf[slot, pl.ds(7, tm), :] + cw[0:1] * ubuf[slot, pl.ds(6, tm), :]
                + cb_ref[:, cs])

    hs = []
    for c in range(nsub):
        cs = slice(c * FF_SUB, (c + 1) * FF_SUB)
        cg = conv(2 * c, cs, cwg_ref, cbg_ref)
        cv = conv(2 * c + 1, cs, cwv_ref, cbv_ref)
        hs.append((jax.nn.gelu(cg) * cv).astype(BF16))
    hmid = jnp.concatenate(hs, axis=1) if nsub > 1 else hs[0]
    acc[...] += jnp.dot(hmid, wd_ref[...], preferred_element_type=F32)

    @pl.when(k == pl.num_programs(1) - 1)
    def _():
        o_ref[...] = xmid[...] + _rms(acc[...], nwb_ref[...])


def _ffn_prompt(x, parts, wout, nwm, nwa, nwb, wup, cw, cb, wdn, *, T, tm, tf):
    N = x.shape[0]
    nk = DFF // tf
    ntiles = N // tm
    row = lambda i, k: (i, 0)
    const = lambda i, k: (0, 0)
    return pl.pallas_call(
        functools.partial(_ffn_prompt_kernel, nparts=len(parts), tm=tm, tiles_per_seq=T // tm),
        grid=(ntiles, nk),
        in_specs=[pl.BlockSpec((tm, D), row)] + [pl.BlockSpec((tm, p.shape[1]), row) for p in parts] + [
                  pl.BlockSpec(wout.shape, const), pl.BlockSpec((1, D), const),
                  pl.BlockSpec((1, D), const), pl.BlockSpec((1, D), const),
                  pl.BlockSpec((D, tf), lambda i, k: (0, k)), pl.BlockSpec((D, tf), lambda i, k: (0, nk + k)),
                  pl.BlockSpec((3, tf), lambda i, k: (0, k)), pl.BlockSpec((3, tf), lambda i, k: (0, nk + k)),
                  pl.BlockSpec((1, tf), lambda i, k: (0, k)), pl.BlockSpec((1, tf), lambda i, k: (0, nk + k)),
                  pl.BlockSpec((tf, D), lambda i, k: (k, 0))],
        out_specs=[pl.BlockSpec((tm, D), row), pl.BlockSpec((1, 2, tf), lambda i, k: (i, 0, k)),
                   pl.BlockSpec((1, 2, tf), lambda i, k: (i, 0, k))],
        out_shape=[jax.ShapeDtypeStruct((N, D), F32), jax.ShapeDtypeStruct((ntiles, 2, DFF), F32),
                   jax.ShapeDtypeStruct((ntiles, 2, DFF), F32)],
        scratch_shapes=[pltpu.VMEM((tm, D), F32), pltpu.VMEM((tm, D), BF16), pltpu.VMEM((tm, D), F32),
                        pltpu.VMEM((nk, 8, tf), F32), pltpu.VMEM((nk, 8, tf), F32),
                        pltpu.VMEM((2 * (tf // FF_SUB), tm + 8, FF_SUB), F32)],
        compiler_params=_cp("arbitrary", "arbitrary"),
        name="ffn_prompt",
    )(x, *parts, wout, nwm, nwa, nwb, wup, wup, cw, cw, cb, cb, wdn)


def _ffn_sample_kernel(*refs, nparts, DB, Ts):
    x_ref = refs[0]
    part_refs = refs[1:1 + nparts]
    (wo_ref, nwm_ref, nwa_ref, nwb_ref, wg_ref, wv_ref, cwg_ref, cwv_ref, cbg_ref, cbv_ref, wd_ref,
     b0g_ref, b0v_ref, b1g_ref, b1v_ref, o_ref, n0g_ref, n0v_ref, n1g_ref, n1v_ref,
     xmid, xn, acc) = refs[1 + nparts:]
    k = pl.program_id(0)

    @pl.when(k == 0)
    def _():
        xm = _mix_residual(x_ref, part_refs, wo_ref, nwm_ref)
        xmid[...] = xm
        xn[...] = _rms(xm, nwa_ref[...]).astype(BF16)
        acc[...] = jnp.zeros(acc.shape, F32)

    def conv(w_ref, cw_ref, cb_ref, b0_ref, b1_ref, n0_ref, n1_ref):
        u = jnp.dot(xn[...], w_ref[...], preferred_element_type=F32)
        xp = [b0_ref[...], b1_ref[...]] + [u[t * DB:(t + 1) * DB] for t in range(Ts)]
        n0_ref[...] = xp[Ts]
        n1_ref[...] = xp[Ts + 1]
        cw = cw_ref[...]
        return jnp.concatenate(
            [cw[0:1] * xp[t] + cw[1:2] * xp[t + 1] + cw[2:3] * xp[t + 2] + cb_ref[...] for t in range(Ts)], axis=0)

    cg = conv(wg_ref, cwg_ref, cbg_ref, b0g_ref, b1g_ref, n0g_ref, n1g_ref)
    cv = conv(wv_ref, cwv_ref, cbv_ref, b0v_ref, b1v_ref, n0v_ref, n1v_ref)
    hmid = jax.nn.gelu(cg) * cv
    acc[...] += jnp.dot(hmid.astype(BF16), wd_ref[...], preferred_element_type=F32)

    @pl.when(k == pl.num_programs(0) - 1)
    def _():
        o_ref[...] = xmid[...] + _rms(acc[...], nwb_ref[...])


def _ffn_sample(x, parts, wout, nwm, nwa, nwb, wup, cw, cb, wdn, fbuf, *, layer, DB, Ts, tf):
    N = DB * Ts
    nk = DFF // tf
    const = lambda k: (0, 0)
    piece = jax.ShapeDtypeStruct((DB, DFF), F32)
    return pl.pallas_call(
        functools.partial(_ffn_sample_kernel, nparts=len(parts), DB=DB, Ts=Ts),
        grid=(nk,),
        in_specs=[pl.BlockSpec((N, D), const)] + [pl.BlockSpec((N, p.shape[1]), const) for p in parts] + [
                  pl.BlockSpec(wout.shape, const), pl.BlockSpec((1, D), const),
                  pl.BlockSpec((1, D), const), pl.BlockSpec((1, D), const),
                  pl.BlockSpec((D, tf), lambda k: (0, k)), pl.BlockSpec((D, tf), lambda k: (0, nk + k)),
                  pl.BlockSpec((3, tf), lambda k: (0, k)), pl.BlockSpec((3, tf), lambda k: (0, nk + k)),
                  pl.BlockSpec((1, tf), lambda k: (0, k)), pl.BlockSpec((1, tf), lambda k: (0, nk + k)),
                  pl.BlockSpec((tf, D), lambda k: (k, 0)),
                  pl.BlockSpec((DB, tf), lambda k: (layer, k)), pl.BlockSpec((DB, tf), lambda k: (layer, nk + k)),
                  pl.BlockSpec((DB, tf), lambda k: (layer, 2 * nk + k)),
                  pl.BlockSpec((DB, tf), lambda k: (layer, 3 * nk + k))],
        out_specs=[pl.BlockSpec((N, D), const)] + [pl.BlockSpec((DB, tf), lambda k: (0, k))] * 4,
        out_shape=[jax.ShapeDtypeStruct((N, D), F32), piece, piece, piece, piece],
        scratch_shapes=[pltpu.VMEM((N, D), F32), pltpu.VMEM((N, D), BF16), pltpu.VMEM((N, D), F32)],
        compiler_params=_cp("arbitrary"),
        name="ffn_sample",
    )(x, *parts, wout, nwm, nwa, nwb, wup, wup, cw, cw, cb, cb, wdn, fbuf, fbuf, fbuf, fbuf)


def _odd_post(c, ba, alog, dtb, g_ref, gt_ref):
    c = c * jax.nn.sigmoid(c)
    q = c[:, 0:KW]
    k = c[:, KW:2 * KW]
    v = c[:, 2 * KW:CCH]
    G = g_ref[...]
    GT = gt_ref[...]
    q = q * _dot2(lax.rsqrt(_dot(q * q, G) + EPS), GT)
    k = k * _dot2(lax.rsqrt(_dot(k * k, G) + EPS), GT)
    lane = lax.broadcasted_iota(jnp.int32, ba.shape, 1)
    beta = jax.nn.sigmoid(ba)
    g = -jnp.exp(alog) * jax.nn.softplus(ba + dtb)
    gates = jnp.where(lane < NH, beta, g)
    return q, k, v, gates


def _odd_in_prompt_kernel(x_ref, nw_ref, win_ref, cw_ref, alog_ref, dtb_ref, g_ref, gt_ref,
                          q_ref, k_ref, v_ref, z_ref, gates_ref, tail_ref, cbuf, *, tm):
    i = pl.program_id(1)
    xn = _rms(x_ref[...], nw_ref[...])
    proj = jnp.dot(xn.astype(BF16), win_ref[...], preferred_element_type=F32)
    qkv = proj[:, 0:CCH]
    z_ref[...] = proj[:, CCH:CCH + VW]
    ba = proj[:, CCH + VW:O_PAD]

    @pl.when(i == 0)
    def _():
        cbuf[0:8, :] = jnp.zeros((8, CCH), F32)

    @pl.when(i > 0)
    def _():
        cbuf[0:8, :] = cbuf[tm:tm + 8, :]

    cbuf[8:tm + 8, :] = qkv
    cw = cw_ref[...]
    c = (cw[3:4] * qkv + cw[2:3] * cbuf[pl.ds(7, tm), :] + cw[1:2] * cbuf[pl.ds(6, tm), :]
         + cw[0:1] * cbuf[pl.ds(5, tm), :])
    q, k, v, gates = _odd_post(c, ba, alog_ref[...], dtb_ref[...], g_ref, gt_ref)
    q_ref[...] = q
    k_ref[...] = k
    v_ref[...] = v
    gates_ref[...] = gates
    tail_ref[0] = cbuf[pl.ds(tm + 5, 3), :]


def _odd_in_prompt(x, nw, win, cw, alog, dtb, G, GT, *, B, T, tm):
    nt = T // tm
    N = B * T
    row = lambda b, i: (b * nt + i, 0)
    const = lambda b, i: (0, 0)
    return pl.pallas_call(
        functools.partial(_odd_in_prompt_kernel, tm=tm),
        grid=(B, nt),
        in_specs=[pl.BlockSpec((tm, D), row), pl.BlockSpec((1, D), const), pl.BlockSpec((D, O_PAD), const),
                  pl.BlockSpec((4, CCH), const), pl.BlockSpec((1, LANES), const), pl.BlockSpec((1, LANES), const),
                  pl.BlockSpec((KW, LANES), const), pl.BlockSpec((LANES, KW), const)],
        out_specs=[pl.BlockSpec((tm, KW), row), pl.BlockSpec((tm, KW), row), pl.BlockSpec((tm, VW), row),
                   pl.BlockSpec((tm, VW), row), pl.BlockSpec((tm, LANES), row),
                   pl.BlockSpec((1, 3, CCH), lambda b, i: (b, 0, 0))],
        out_shape=[jax.ShapeDtypeStruct((N, KW), F32), jax.ShapeDtypeStruct((N, KW), F32),
                   jax.ShapeDtypeStruct((N, VW), F32), jax.ShapeDtypeStruct((N, VW), F32),
                   jax.ShapeDtypeStruct((N, LANES), F32), jax.ShapeDtypeStruct((B, 3, CCH), F32)],
        scratch_shapes=[pltpu.VMEM((tm + 8, CCH), F32)],
        compiler_params=_cp("arbitrary", "arbitrary"),
        name="odd_in_prompt",
    )(x, nw, win, cw, alog, dtb, G, GT)


def _odd_in_sample_kernel(x_ref, nw_ref, win_ref, cbuf_ref, cw_ref, alog_ref, dtb_ref, g_ref, gt_ref,
                          q_ref, k_ref, v_ref, z_ref, gates_ref, tail_ref, *, DB, Ts):
    xn = _rms(x_ref[...], nw_ref[...])
    proj = jnp.dot(xn.astype(BF16), win_ref[...], preferred_element_type=F32)
    qkv = proj[:, 0:CCH]
    z_ref[...] = proj[:, CCH:CCH + VW]
    ba = proj[:, CCH + VW:O_PAD]
    xp = [cbuf_ref[0], cbuf_ref[1], cbuf_ref[2]] + [qkv[t * DB:(t + 1) * DB] for t in range(Ts)]
    cw = cw_ref[...]
    c = jnp.concatenate(
        [cw[0:1] * xp[t] + cw[1:2] * xp[t + 1] + cw[2:3] * xp[t + 2] + cw[3:4] * xp[t + 3] for t in range(Ts)],
        axis=0)
    q, k, v, gates = _odd_post(c, ba, alog_ref[...], dtb_ref[...], g_ref, gt_ref)
    q_ref[...] = q
    k_ref[...] = k
    v_ref[...] = v
    gates_ref[...] = gates
    for r in range(3):
        tail_ref[r] = xp[Ts + r]


def _odd_in_sample(x, nw, win, cbuf, cw, alog, dtb, G, GT, *, DB, Ts):
    N = DB * Ts
    return pl.pallas_call(
        functools.partial(_odd_in_sample_kernel, DB=DB, Ts=Ts),
        out_shape=[jax.ShapeDtypeStruct((N, KW), F32), jax.ShapeDtypeStruct((N, KW), F32),
                   jax.ShapeDtypeStruct((N, VW), F32), jax.ShapeDtypeStruct((N, VW), F32),
                   jax.ShapeDtypeStruct((N, LANES), F32), jax.ShapeDtypeStruct((3, DB, CCH), F32)],
        compiler_params=pltpu.CompilerParams(vmem_limit_bytes=VMEM_LIMIT),
        name="odd_in_sample",
    )(x, nw, win, cbuf, cw, alog, dtb, G, GT)


def _delta_kernel(q_ref, k_ref, v_ref, z_ref, gates_ref, s0_ref, nw_ref, o_ref, sout_ref, S, *, C, nb):
    c_idx = pl.program_id(1)

    @pl.when(c_idx == 0)
    def _():
        S[...] = s0_ref[...]

    row = lax.broadcasted_iota(jnp.int32, (C, C), 0)
    col = lax.broadcasted_iota(jnp.int32, (C, C), 1)
    incl = row >= col
    strict = row > col
    nw = nw_ref[...]
    chains = [(b, h) for b in range(nb) for h in range(NH)]
    gates, gcs, gcs_t = [], [], []
    for b in range(nb):
        g = gates_ref[b]
        gc_all = _cumsum_rows(g)
        gpad = jnp.concatenate([gc_all, jnp.zeros((LANES - C, LANES), F32)], axis=0) if C < LANES else gc_all
        gates.append(g)
        gcs.append(gc_all)
        gcs_t.append(gpad.T)
    eye = (row == col).astype(F32)
    qs, kss, gc_, decay_, lms, rhss = [], [], [], [], [], []
    for b, h in chains:
        qh = q_ref[b, :, h * DK:(h + 1) * DK] * (DK ** -0.5)
        kh = k_ref[b, :, h * DK:(h + 1) * DK]
        vh = v_ref[b, :, h * DV:(h + 1) * DV]
        beta = gates[b][:, h:h + 1]
        gc = gcs[b][:, NH + h:NH + h + 1]
        gr = gcs_t[b][NH + h:NH + h + 1, 0:C]
        decay = jnp.where(incl, jnp.exp(jnp.where(incl, gc - gr, 0.0)), 0.0)
        kb = kh * beta
        lm = jnp.where(strict, _dot_nt(kb, kh) * decay, 0.0)
        rhs = jnp.concatenate([vh * beta, kb * jnp.exp(gc)], axis=-1)
        qs.append(qh); kss.append(kh); gc_.append(gc); decay_.append(decay)
        lms.append(lm)
        rhss.append(rhs)
    pws = [-lm for lm in lms]
    invs = [eye + p for p in pws]
    n = 2
    while n < C:
        pws = [_dot(p, p) for p in pws]
        invs = [x + _dot(p, x) for p, x in zip(pws, invs)]
        n *= 2
    sols = [_dot(x, r) for x, r in zip(invs, rhss)]
    for _ in range(2):
        res = [r - (s + _dot3(lm, s)) for r, s, lm in zip(rhss, sols, lms)]
        sols = [s + _dot(x, d) for s, x, d in zip(sols, invs, res)]
    attns = [_dot_nt(qh, kh) * d for qh, kh, d in zip(qs, kss, decay_)]
    Ss = [S[b, h] for b, h in chains]
    v_news = [sol[:, 0:DV] - _dot(sol[:, DV:DV + DK], Sh) for sol, Sh in zip(sols, Ss)]
    os_ = [_dot(qh * jnp.exp(gc), Sh) + _dot(at, vn)
           for qh, gc, Sh, at, vn in zip(qs, gc_, Ss, attns, v_news)]
    for (b, h), kh, gc, Sh, vn, o in zip(chains, kss, gc_, Ss, v_news, os_):
        g_last = gc[C - 1:C, :]
        S[b, h] = Sh * jnp.exp(g_last) + _dot_tn(kh * jnp.exp(g_last - gc), vn)
        zh = z_ref[b, :, h * DV:(h + 1) * DV]
        o_ref[b, :, h * DV:(h + 1) * DV] = _rms(o, nw) * (zh * jax.nn.sigmoid(zh))

    @pl.when(c_idx == pl.num_programs(1) - 1)
    def _():
        sout_ref[...] = S[...]


def _delta(q, k, v, z, gates, s0, nw, *, nseq, nchunks, C, nb, s_base):
    Tq = nchunks * C
    blk = lambda ch: pl.BlockSpec((nb, C, ch), lambda b, c: (b, c, 0))
    return pl.pallas_call(
        functools.partial(_delta_kernel, C=C, nb=nb),
        grid=(nseq // nb, nchunks),
        in_specs=[blk(KW), blk(KW), blk(VW), blk(VW), blk(LANES),
                  pl.BlockSpec((nb, NH, DK, DV), lambda b, c: (s_base // nb + b, 0, 0, 0)),
                  pl.BlockSpec((1, DV), lambda b, c: (0, 0))],
        out_specs=[blk(VW), pl.BlockSpec((nb, NH, DK, DV), lambda b, c: (b, 0, 0, 0))],
        out_shape=[jax.ShapeDtypeStruct((nseq, Tq, VW), F32), jax.ShapeDtypeStruct((nseq, NH, DK, DV), F32)],
        scratch_shapes=[pltpu.VMEM((nb, NH, DK, DV), F32)],
        compiler_params=_cp("arbitrary", "arbitrary"),
        name="delta",
    )(q, k, v, z, gates, s0, nw)


def _tmajor_to_seq(a, DB, Ts):
    return a.reshape(Ts, DB, a.shape[-1]).transpose(1, 0, 2)


def _pad_seq(a, DB, Ts, Tp):
    s = _tmajor_to_seq(a, DB, Ts)
    return jnp.pad(s, ((0, 0), (0, Tp - Ts), (0, 0)))


def kernel(x_prompt, x_sample, cache_sb_k, cache_sb_v, state_lru_h, state_lru_conv, state_dn_S, state_dn_conv,
           state_ffn_conv, page_table, norm_mix_pre, norm_mix_post, norm_ffn_pre, norm_ffn_post, w_in_e,
           lru_conv_w, lru_conv_b, lru_wa, lru_ba, lru_wx, lru_bx, lru_lambda, sb_bias, w_out_e, w_in_o,
           dn_conv_w, dn_A_log, dn_dt_bias, dn_norm_w, w_out_o, ffn_w_up, ffn_conv_w, ffn_conv_b, ffn_w_down):
    B, T, d_model = x_prompt.shape
    DB, Ts, _ = x_sample.shape
    depth = norm_mix_pre.shape[0]
    n_even, n_pool, page = cache_sb_k.shape[0], cache_sb_k.shape[1], cache_sb_k.shape[2]
    n_pages = page_table.shape[1]
    assert d_model == D and w_in_e.shape[-1] == E_IN and w_in_o.shape[-1] == O_IN
    assert ffn_w_down.shape[1] == DFF and n_pages * page > 0 and Ts >= 3
    tm = min(512, T)
    tq = min(256, T)
    tf = 1024
    C = min(DN_CHUNK, T)
    Cs = 8
    pps = next(p for p in (16, 8, 4, 2, 1) if n_pages % p == 0)
    nbp = 2 if B % 2 == 0 else 1
    nbs = next(n for n in (4, 2, 1) if DB % n == 0)
    assert T % tm == 0 and T % tq == 0 and T % C == 0 and Ts <= Cs and (DB * Ts) % 8 == 0

    xp = x_prompt.reshape(B * T, D)
    xs = x_sample.transpose(1, 0, 2).reshape(Ts * DB, D)
    pool_k = cache_sb_k.transpose(0, 1, 3, 4, 2)
    pool_v = cache_sb_v.transpose(0, 1, 3, 4, 2)
    dn_S_all = state_dn_S.reshape(-1, NH, DK, DV)
    fbuf_all = state_ffn_conv.reshape(depth * DB, 2 * 2 * DFF)
    s0_zero = jnp.zeros((B, NH, DK, DV), F32)

    G = jnp.asarray((np.arange(KW)[:, None] // DK == np.arange(LANES)[None, :]).astype(np.float32), BF16)
    GT = G.T

    lane_in_head = np.arange(NH * LANES) % LANES
    kones = jnp.asarray(((lane_in_head == HD) | (lane_in_head == HD + 1)).astype(np.float32)[None, :])

    r1 = lambda a: a.reshape(1, -1)
    ks, vs, hs, lcs, Ss, dcs, fcs = ([[], []] for _ in range(7))
    for l in range(depth):
        j = l // 2
        nw_pre = r1(norm_mix_pre[l])
        if l % 2 == 0:
            win = w_in_e[j].astype(BF16)
            wa = jax.scipy.linalg.block_diag(*lru_wa[j]).astype(BF16)
            wx = jax.scipy.linalg.block_diag(*lru_wx[j]).astype(BF16)
            pe = (lru_conv_w[j], r1(lru_conv_b[j]), wa, r1(lru_ba[j]), wx, r1(lru_bx[j]), r1(lru_lambda[j]))
            wkvt = w_in_e[j][:, 2 * DR + SBW:].T.astype(BF16)
            wk = w_in_e[j][:, 2 * DR + SBW:2 * DR + 2 * SBW].reshape(D, NH, HD)
            wk128 = jnp.pad(wk, ((0, 0), (0, 0), (0, LANES - HD))).reshape(D, NH * LANES).astype(BF16)
            yr_p, q_p, kt_p, vt_p, k16_p, v16_p, hl_p, lc_p = _even_in_prompt(xp, nw_pre, win, wkvt, wk128, kones,
                                                                              *pe, B=B, T=T, tm=tm)
            ya_p = _attn_prompt(q_p, k16_p, v16_p, sb_bias[j], B=B, T=T, tq=tq)
            yr_s, q_s, k_s, v_s, hl_s, lc_s = _even_in_sample(
                xs, nw_pre, win, state_lru_conv[j].transpose(1, 0, 2), state_lru_h[j], *pe, DB=DB, Ts=Ts)
            q_s, k_s, v_s = (_tmajor_to_seq(a, DB, Ts) for a in (q_s, k_s, v_s))
            ya_s = _attn_sample(q_s, k_s, v_s, sb_bias[j], pool_k, pool_v, page_table, layer=j,
                                DB=DB, Ts=Ts, pps=pps)
            ya_s = ya_s.transpose(1, 0, 2).reshape(Ts * DB, SBW)
            wout = w_out_e[j].astype(BF16)
            parts_p, parts_s = [yr_p, ya_p], [yr_s, ya_s]
            tok_major = lambda a: a.reshape(B, NH, HD, T).transpose(0, 3, 1, 2)
            ks[0].append(tok_major(kt_p)); ks[1].append(k_s.reshape(DB, Ts, NH, HD))
            vs[0].append(tok_major(vt_p)); vs[1].append(v_s.reshape(DB, Ts, NH, HD))
            hs[0].append(hl_p.reshape(B, DR)); hs[1].append(hl_s)
            lcs[0].append(lc_p); lcs[1].append(lc_s.transpose(1, 0, 2))
        else:
            win = jnp.pad(w_in_o[j], ((0, 0), (0, O_PAD - O_IN))).astype(BF16)
            alog = jnp.zeros((1, LANES), F32).at[0, NH:2 * NH].set(dn_A_log[j])
            dtb = jnp.zeros((1, LANES), F32).at[0, NH:2 * NH].set(dn_dt_bias[j])
            nwd = r1(dn_norm_w[j])
            q_p, k_p, v_p, z_p, g_p, dc_p = _odd_in_prompt(xp, nw_pre, win, dn_conv_w[j], alog, dtb, G, GT,
                                                           B=B, T=T, tm=tm)
            sq = lambda a: a.reshape(B, T, a.shape[-1])
            o_p, S_p = _delta(sq(q_p), sq(k_p), sq(v_p), sq(z_p), sq(g_p), s0_zero, nwd, nseq=B, nchunks=T // C,
                              C=C, nb=nbp, s_base=0)
            o_p = o_p.reshape(B * T, VW)
            q_s, k_s, v_s, z_s, g_s, dc_s = _odd_in_sample(
                xs, nw_pre, win, state_dn_conv[j].transpose(1, 0, 2), dn_conv_w[j], alog, dtb, G, GT, DB=DB, Ts=Ts)
            q_s, k_s, v_s, z_s, g_s = (_pad_seq(a, DB, Ts, Cs) for a in (q_s, k_s, v_s, z_s, g_s))
            o_s, S_s = _delta(q_s, k_s, v_s, z_s, g_s, dn_S_all, nwd, nseq=DB, nchunks=1, C=Cs, nb=nbs,
                              s_base=j * DB)
            o_s = o_s[:, :Ts].transpose(1, 0, 2).reshape(Ts * DB, VW)
            wout = w_out_o[j].astype(BF16)
            parts_p, parts_s = [o_p], [o_s]
            Ss[0].append(S_p); Ss[1].append(S_s)
            dcs[0].append(dc_p); dcs[1].append(dc_s.transpose(1, 0, 2))
        wup = ffn_w_up[l].astype(BF16)
        wdn = ffn_w_down[l].astype(BF16)
        fa = (wout, r1(norm_mix_post[l]), r1(norm_ffn_pre[l]), r1(norm_ffn_post[l]), wup, ffn_conv_w[l],
              r1(ffn_conv_b[l]), wdn)
        xp, tg, tv = _ffn_prompt(xp, parts_p, *fa, T=T, tm=tm, tf=tf)
        last = (np.arange(B) + 1) * (T // tm) - 1
        fcs[0].append(jnp.concatenate([tg[last], tv[last]], axis=-1))
        xs, n0g, n0v, n1g, n1v = _ffn_sample(xs, parts_s, *fa, fbuf_all, layer=l, DB=DB, Ts=Ts, tf=tf)
        fcs[1].append(jnp.stack([jnp.concatenate([n0g, n0v], axis=-1), jnp.concatenate([n1g, n1v], axis=-1)], axis=1))

    y_prompt = xp.reshape(B, T, D)
    y_sample = xs.reshape(Ts, DB, D).transpose(1, 0, 2)
    st = jnp.stack
    return (y_prompt, y_sample, st(ks[0]), st(vs[0]), st(ks[1]), st(vs[1]), st(hs[0]), st(hs[1]),
            st(lcs[0]), st(lcs[1]), st(Ss[0]), st(Ss[1]), st(dcs[0]), st(dcs[1]), st(fcs[0]), st(fcs[1]))
```
